```python
import math
import jax, jax.numpy as jnp
from jax import lax
import numpy as np

D_MODEL = 1024
BATCH = 4
SEQ = 8192
DEPTH = 1

MEM_LEN = 256
MLA_HEADS = 8
MLA_Q_RANK = 256
MLA_KV_RANK = 128
MLA_NOPE = 64
MLA_ROPE = 32
MLA_V = 64
ROPE_THETA = 10000.0
SB_HEADS = 8
SB_HEAD_DIM = 64
X_HEADS = 4
X_HEAD_DIM = 128
D_FF = -(-8 * D_MODEL // (3 * 256)) * 256
Q_BLOCK = 128
EPS = 1e-6
SB_WIDTH = SB_HEADS * SB_HEAD_DIM
IN_SPLITS = (MLA_Q_RANK, MLA_KV_RANK, MLA_ROPE, SB_WIDTH, SB_WIDTH, SB_WIDTH, D_MODEL, D_MODEL)
D_IN = MLA_Q_RANK + MLA_KV_RANK + MLA_ROPE + 3 * SB_WIDTH + 2 * D_MODEL

kernel_name = "hybrid_mla_stickbreaking_gated_block"


def rms_norm(x, g):
    xf = x.astype(jnp.float32)
    y = xf * lax.rsqrt(jnp.mean(xf * xf, axis=-1, keepdims=True) + EPS)
    return (y * g.astype(jnp.float32)).astype(x.dtype)


def rope_cos_sin(positions):
    inv_freq = ROPE_THETA ** (-jnp.arange(0, MLA_ROPE, 2, dtype=jnp.float32) / MLA_ROPE)
    ang = positions.astype(jnp.float32)[..., None] * inv_freq
    return jnp.cos(ang), jnp.sin(ang)


def apply_rope(x, cos, sin):
    half = x.shape[-1] // 2
    x1 = x[..., :half].astype(jnp.float32)
    x2 = x[..., half:].astype(jnp.float32)
    out = jnp.concatenate([x1 * cos - x2 * sin, x2 * cos + x1 * sin], axis=-1)
    return out.astype(x.dtype)


def mla_block(qb, blk, k, v):
    scale = 1.0 / math.sqrt(MLA_NOPE + MLA_ROPE)
    s = jnp.einsum('bqhd,bkhd->bhqk', qb, k).astype(jnp.float32) * scale
    q_pos = blk * Q_BLOCK + jnp.arange(Q_BLOCK)
    k_pos = jnp.arange(k.shape[1])
    mask = k_pos[None, :] <= q_pos[:, None]
    s = jnp.where(mask, s, jnp.finfo(jnp.float32).min)
    p = jax.nn.softmax(s, axis=-1).astype(v.dtype)
    return jnp.einsum('bhqk,bkhd->bqhd', p, v)


def stick_breaking_block(qb, blk, k, v):
    scale = 1.0 / math.sqrt(SB_HEAD_DIM)
    z = jnp.einsum('bqhd,bkhd->bhqk', qb, k).astype(jnp.float32) * scale
    q_pos = blk * Q_BLOCK + jnp.arange(Q_BLOCK)
    k_pos = jnp.arange(k.shape[1])
    mask = k_pos[None, :] < q_pos[:, None]
    log_beta = jax.nn.log_sigmoid(z)
    log_one_minus = jnp.where(mask, jax.nn.log_sigmoid(-z), 0.0)
    suffix = lax.cumsum(log_one_minus, axis=3, reverse=True) - log_one_minus
    a = jnp.where(mask, jnp.exp(log_beta + suffix), 0.0).astype(v.dtype)
    return jnp.einsum('bhqk,bkhd->bqhd', a, v)


def blocked_attend(block_fn, q, k, v):
    b, s, h, dq = q.shape
    dv = v.shape[-1]
    nb = s // Q_BLOCK
    qb = q.reshape(b, nb, Q_BLOCK, h, dq).transpose(1, 0, 2, 3, 4)
    out = lax.map(lambda a: block_fn(a[0], a[1], k, v), (qb, jnp.arange(nb)))
    return out.transpose(1, 0, 2, 3, 4).reshape(b, s, h * dv)


def setup_inputs(seed: int = 0) -> dict:
    key = jax.random.key(seed)
    ks = jax.random.split(key, 24)

    def w(k, shape, fan_in):
        return jax.random.normal(k, shape, jnp.float32) * (fan_in ** -0.5)

    def gain(k, shape):
        return 1.0 + 0.01 * jax.random.normal(k, shape, jnp.float32)

    L = DEPTH
    return {
        "x": jax.random.normal(ks[0], (BATCH, SEQ, D_MODEL), jnp.float32),
        "mem": jax.random.normal(ks[1], (BATCH, MEM_LEN, D_MODEL), jnp.float32),
        "positions": jnp.broadcast_to(jnp.arange(SEQ, dtype=jnp.int32)[None, :], (BATCH, SEQ)),
        "g_mix": gain(ks[2], (L, D_MODEL)),
        "w_in": w(ks[3], (L, D_MODEL, D_IN), D_MODEL),
        "b_gate": 0.01 * jax.random.normal(ks[4], (L, 2, D_MODEL), jnp.float32),
        "g_q_lat": gain(ks[5], (L, MLA_Q_RANK)),
        "w_uq": w(ks[6], (L, MLA_Q_RANK, MLA_HEADS * (MLA_NOPE + MLA_ROPE)), MLA_Q_RANK),
        "g_kv_lat": gain(ks[7], (L, MLA_KV_RANK)),
        "w_ukv": w(ks[8], (L, MLA_KV_RANK, MLA_HEADS * (MLA_NOPE + MLA_V)), MLA_KV_RANK),
        "w_a_proj": w(ks[9], (L, MLA_HEADS * MLA_V, D_MODEL), MLA_HEADS * MLA_V),
        "w_b_proj": w(ks[10], (L, SB_WIDTH, D_MODEL), SB_WIDTH),
        "w_o": w(ks[11], (L, D_MODEL, D_MODEL), D_MODEL),
        "g_x": gain(ks[12], (L, D_MODEL)),
        "g_mem": gain(ks[13], (L, D_MODEL)),
        "w_xq": w(ks[14], (L, D_MODEL, X_HEADS * X_HEAD_DIM), D_MODEL),
        "w_xkv": w(ks[15], (L, D_MODEL, 2 * X_HEADS * X_HEAD_DIM), D_MODEL),
        "w_xo": w(ks[16], (L, X_HEADS * X_HEAD_DIM, D_MODEL), X_HEADS * X_HEAD_DIM),
        "g_ffn": gain(ks[17], (L, D_MODEL)),
        "w_gate": w(ks[18], (L, D_MODEL, D_FF), D_MODEL),
        "w_up": w(ks[19], (L, D_MODEL, D_FF), D_MODEL),
        "w_down": w(ks[20], (L, D_FF, D_MODEL), D_FF),
        "g_final": gain(ks[21], (D_MODEL,)),
    }


def reference(x, mem, positions, g_mix, w_in, b_gate, g_q_lat, w_uq, g_kv_lat, w_ukv,
              w_a_proj, w_b_proj, w_o, g_x, g_mem, w_xq, w_xkv, w_xo,
              g_ffn, w_gate, w_up, w_down, g_final):
    b, s, _ = x.shape
    m_len = mem.shape[1]
    cos, sin = rope_cos_sin(positions)
    split_points = list(np.cumsum(IN_SPLITS)[:-1])

    for l in range(DEPTH):
        h = rms_norm(x, g_mix[l])
        proj = h @ w_in[l]
        c_q, c_kv, k_r, sb_q, sb_k, sb_v, gate_a, gate_b = jnp.split(proj, split_points, axis=-1)

        q = (rms_norm(c_q, g_q_lat[l]) @ w_uq[l]).reshape(b, s, MLA_HEADS, MLA_NOPE + MLA_ROPE)
        q_rope = apply_rope(q[..., MLA_NOPE:], cos[:, :, None, :], sin[:, :, None, :])
        q_a = jnp.concatenate([q[..., :MLA_NOPE], q_rope], axis=-1)
        kv = (rms_norm(c_kv, g_kv_lat[l]) @ w_ukv[l]).reshape(b, s, MLA_HEADS, MLA_NOPE + MLA_V)
        k_r = apply_rope(k_r, cos, sin)
        k_a = jnp.concatenate(
            [kv[..., :MLA_NOPE], jnp.broadcast_to(k_r[:, :, None, :], (b, s, MLA_HEADS, MLA_ROPE))], axis=-1)
        v_a = kv[..., MLA_NOPE:]
        o_a = blocked_attend(mla_block, q_a, k_a, v_a)

        o_b = blocked_attend(
            stick_breaking_block,
            sb_q.reshape(b, s, SB_HEADS, SB_HEAD_DIM),
            sb_k.reshape(b, s, SB_HEADS, SB_HEAD_DIM),
            sb_v.reshape(b, s, SB_HEADS, SB_HEAD_DIM))

        merged = (jax.nn.sigmoid(gate_a + b_gate[l, 0]) * (o_a @ w_a_proj[l])
                  + jax.nn.sigmoid(gate_b + b_gate[l, 1]) * (o_b @ w_b_proj[l]))
        x = x + merged @ w_o[l]

        hx = rms_norm(x, g_x[l])
        mn = rms_norm(mem, g_mem[l])
        xq = (hx @ w_xq[l]).reshape(b, s, X_HEADS, X_HEAD_DIM)
        xkv = (mn @ w_xkv[l]).reshape(b, m_len, 2, X_HEADS, X_HEAD_DIM)
        xs = jnp.einsum('bqhd,bkhd->bhqk', xq, xkv[:, :, 0]).astype(jnp.float32) / math.sqrt(X_HEAD_DIM)
        xp = jax.nn.softmax(xs, axis=-1).astype(x.dtype)
        xo = jnp.einsum('bhqk,bkhd->bqhd', xp, xkv[:, :, 1]).reshape(b, s, X_HEADS * X_HEAD_DIM)
        x = x + xo @ w_xo[l]

        hf = rms_norm(x, g_ffn[l])
        x = x + (jax.nn.silu(hf @ w_gate[l]) * (hf @ w_up[l])) @ w_down[l]

    return rms_norm(x, g_final)
```

```python
import functools
import math

import jax
import jax.numpy as jnp
from jax import lax
from jax.experimental import pallas as pl
from jax.experimental.pallas import tpu as pltpu

F32 = jnp.float32
BF16 = jnp.bfloat16

EPS = 1e-6
ROPE_THETA = 10000.0
LOG2E = math.log2(math.e)

MLA_HEADS = 8
MLA_Q_RANK = 256
MLA_KV_RANK = 128
MLA_NOPE = 64
MLA_ROPE = 32
MLA_V = 64
SB_HEADS = 8
SB_HEAD_DIM = 64
X_HEADS = 4
X_HEAD_DIM = 128
LANES = 128
NEG_BIG = -1e30

VMEM_LIMIT = 56 * 1024 * 1024


def _rms(x, g):
    ms = jnp.mean(x * x, axis=-1, keepdims=True)
    return x * lax.rsqrt(ms + EPS) * g


def _dot(a, b):
    return jnp.dot(a, b, preferred_element_type=F32)


def _dot_nt(a, b):
    return lax.dot_general(a, b, (((1,), (1,)), ((), ())), preferred_element_type=F32)


_C_CQ = 0
_C_CKV = _C_CQ + MLA_Q_RANK
_C_KR = _C_CKV + MLA_KV_RANK
_C_KRR = _C_KR + LANES
_C_SBQ = _C_KRR + LANES
_SBW = SB_HEADS * SB_HEAD_DIM
_C_SBK = _C_SBQ + _SBW
_C_SBV = _C_SBK + _SBW
_C_GA = _C_SBV + _SBW


def _in_proj_kernel(x_ref, pos_ref, invf_ref, gmix_ref, win_ref, bg_ref, gq_ref, wuq1_ref, wuq2_ref,
                    gkv_ref, wk_ref, wv_ref,
                    q_ref, k_ref, v_ref, sbq_ref, sbk_ref, sbv_ref, ga_ref, gb_ref, *, d_model):
    x = x_ref[...]
    h = _rms(x, gmix_ref[...]).astype(BF16)

    ang = pos_ref[...].astype(F32) * invf_ref[...]
    cos = jnp.cos(ang)
    sin = jnp.sin(ang)

    def proj(c0, width):
        return _dot(h, win_ref[:, c0:c0 + width])

    cq = _rms(proj(_C_CQ, MLA_Q_RANK), gq_ref[...]).astype(BF16)
    q1 = _dot(cq, wuq1_ref[...])
    q2 = _dot(cq, wuq2_ref[...])
    q_scale = LOG2E / math.sqrt(MLA_NOPE + MLA_ROPE)
    for hd in range(MLA_HEADS):
        sl = slice(hd * LANES, (hd + 1) * LANES)
        q_ref[:, sl] = ((q1[:, sl] * cos + q2[:, sl] * sin) * q_scale).astype(BF16)

    ckv = _rms(proj(_C_CKV, MLA_KV_RANK), gkv_ref[...]).astype(BF16)
    kn = _dot(ckv, wk_ref[...])
    v_ref[...] = _dot(ckv, wv_ref[...]).astype(BF16)
    krope = proj(_C_KR, LANES) * cos + proj(_C_KRR, LANES) * sin
    for hd in range(MLA_HEADS):
        sl = slice(hd * LANES, (hd + 1) * LANES)
        k_ref[:, sl] = (kn[:, sl] + krope).astype(BF16)

    sbq_ref[...] = (proj(_C_SBQ, _SBW) * (1.0 / math.sqrt(SB_HEAD_DIM))).astype(BF16)
    sbk_ref[...] = proj(_C_SBK, _SBW).astype(BF16)
    sbv_ref[...] = proj(_C_SBV, _SBW).astype(BF16)

    ga_ref[...] = jax.nn.sigmoid(proj(_C_GA, d_model) + bg_ref[0:1, :]).astype(BF16)
    gb_ref[...] = jax.nn.sigmoid(proj(_C_GA + d_model, d_model) + bg_ref[1:2, :]).astype(BF16)


def _pack_in_weights(w_in, w_uq, w_ukv, d_model):
    f = w_in.dtype
    zeros = lambda n: jnp.zeros((d_model, n), f)
    o_kr = MLA_Q_RANK + MLA_KV_RANK
    o_sb = o_kr + MLA_ROPE
    half = MLA_ROPE // 2
    kr = w_in[:, o_kr:o_kr + MLA_ROPE]
    kr_rot = jnp.concatenate([-kr[:, half:], kr[:, :half]], axis=1)
    pad_tail = LANES - MLA_NOPE - MLA_ROPE
    w_packed = jnp.concatenate([
        w_in[:, :o_kr],
        zeros(MLA_NOPE), kr, zeros(pad_tail),
        zeros(MLA_NOPE), kr_rot, zeros(pad_tail),
        w_in[:, o_sb:],
    ], axis=1).astype(BF16)

    r = w_uq.shape[0]
    wq = w_uq.reshape(r, MLA_HEADS, MLA_NOPE + MLA_ROPE)
    rope = wq[:, :, MLA_NOPE:]
    rope_rot = jnp.concatenate([-rope[:, :, half:], rope[:, :, :half]], axis=2)
    zq = jnp.zeros((r, MLA_HEADS, pad_tail), f)
    wuq1 = jnp.concatenate([wq, zq], axis=2).reshape(r, MLA_HEADS * LANES).astype(BF16)
    wuq2 = jnp.concatenate([jnp.zeros((r, MLA_HEADS, MLA_NOPE), f), rope_rot, zq], axis=2)
    wuq2 = wuq2.reshape(r, MLA_HEADS * LANES).astype(BF16)

    rk = w_ukv.shape[0]
    wkv = w_ukv.reshape(rk, MLA_HEADS, MLA_NOPE + MLA_V)
    wk = jnp.concatenate([wkv[:, :, :MLA_NOPE], jnp.zeros((rk, MLA_HEADS, LANES - MLA_NOPE), f)], axis=2)
    wk = wk.reshape(rk, MLA_HEADS * LANES).astype(BF16)
    wv = wkv[:, :, MLA_NOPE:].reshape(rk, MLA_HEADS * MLA_V).astype(BF16)
    return w_packed, wuq1, wuq2, wk, wv


def _full(shape):
    n = len(shape)
    return pl.BlockSpec(shape, lambda *_: (0,) * n)


def _in_proj(x2, pos2, invf, g_mix, w_packed, b_gate, g_q, wuq1, wuq2, g_kv, wk, wv, tm):
    t, d = x2.shape
    row = lambda w: pl.BlockSpec((tm, w), lambda i: (i, 0))
    out_widths = [MLA_HEADS * LANES, MLA_HEADS * LANES, MLA_HEADS * MLA_V, _SBW, _SBW, _SBW, d, d]
    return pl.pallas_call(
        functools.partial(_in_proj_kernel, d_model=d),
        out_shape=[jax.ShapeDtypeStruct((t, w), BF16) for w in out_widths],
        grid=(t // tm,),
        in_specs=[row(d), row(1), _full(invf.shape), _full(g_mix.shape), _full(w_packed.shape),
                  _full(b_gate.shape), _full(g_q.shape), _full(wuq1.shape), _full(wuq2.shape),
                  _full(g_kv.shape), _full(wk.shape), _full(wv.shape)],
        out_specs=[row(w) for w in out_widths],
        compiler_params=pltpu.CompilerParams(dimension_semantics=("parallel",), vmem_limit_bytes=VMEM_LIMIT),
        name="in_proj",
    )(x2, pos2, invf, g_mix, w_packed, b_gate, g_q, wuq1, wuq2, g_kv, wk, wv)


def _mla_kernel(q_ref, k_ref, v_ref, o_ref, m_sc, l_sc, acc_sc, *, tq, kb):
    qi = pl.program_id(2)
    nd = tq // kb
    rows = lax.broadcasted_iota(jnp.int32, (tq, kb), 0)
    cols = lax.broadcasted_iota(jnp.int32, (tq, kb), 1)
    outs = []
    for j in range(2):
        hs = slice(j * LANES, (j + 1) * LANES)
        q = q_ref[:, hs]
        m_sc[...] = jnp.full(m_sc.shape, NEG_BIG, F32)
        l_sc[...] = jnp.zeros(l_sc.shape, F32)
        acc_sc[...] = jnp.zeros(acc_sc.shape, F32)

        def step(blk, diag):
            start = pl.multiple_of(blk * kb, kb)
            k = k_ref[pl.ds(start, kb), hs]
            v = v_ref[pl.ds(start, kb), :]
            s = _dot_nt(q, k)
            if diag is not None:
                s = jnp.where(cols + diag * kb <= rows, s, NEG_BIG)
            m_prev = m_sc[...]
            m_new = jnp.maximum(m_prev, jnp.max(s, axis=-1, keepdims=True))
            alpha = jnp.exp2(m_prev - m_new)
            p = jnp.exp2(s - m_new)
            l_sc[...] = alpha * l_sc[...] + jnp.sum(p, axis=-1, keepdims=True)
            acc_sc[...] = alpha * acc_sc[...] + _dot(p.astype(BF16), v)
            m_sc[...] = m_new

        def body(blk, carry):
            step(blk, None)
            return carry

        lax.fori_loop(0, qi * nd, body, 0)
        for d in range(nd):
            step(qi * nd + d, d)
        outs.append(acc_sc[...] / l_sc[...])
    lane = lax.broadcasted_iota(jnp.int32, (tq, LANES), 1)
    o_ref[...] = jnp.where(lane < MLA_V, outs[0], outs[1]).astype(o_ref.dtype)


def _mla_attn(q, k, v, tq, kb):
    b, s, _ = q.shape
    hp = MLA_HEADS // 2
    return pl.pallas_call(
        functools.partial(_mla_kernel, tq=tq, kb=kb),
        out_shape=jax.ShapeDtypeStruct((b, s, MLA_HEADS * MLA_V), BF16),
        grid=(b, hp, s // tq),
        in_specs=[pl.BlockSpec((None, tq, 2 * LANES), lambda bi, h, i: (bi, i, h)),
                  pl.BlockSpec((None, s, 2 * LANES), lambda bi, h, i: (bi, 0, h)),
                  pl.BlockSpec((None, s, LANES), lambda bi, h, i: (bi, 0, h))],
        out_specs=pl.BlockSpec((None, tq, LANES), lambda bi, h, i: (bi, i, h)),
        scratch_shapes=[pltpu.VMEM((tq, 1), F32), pltpu.VMEM((tq, 1), F32), pltpu.VMEM((tq, LANES), F32)],
        compiler_params=pltpu.CompilerParams(dimension_semantics=("parallel", "parallel", "parallel"),
                                             vmem_limit_bytes=VMEM_LIMIT),
        name="mla_attn",
    )(q, k, v)


def _sb_kernel(q_ref, k_ref, v_ref, tri_ref, o_ref, r_sc, acc_sc, *, tq, kb):
    qi = pl.program_id(2)
    nd = tq // kb
    rows = lax.broadcasted_iota(jnp.int32, (tq, kb), 0)
    cols = lax.broadcasted_iota(jnp.int32, (tq, kb), 1)
    lane = lax.broadcasted_iota(jnp.int32, (tq, LANES), 1)
    q_both = q_ref[...]
    outs = []
    for j in range(2):
        in_head = (lane >= j * SB_HEAD_DIM) & (lane < (j + 1) * SB_HEAD_DIM)
        q = jnp.where(in_head, q_both, jnp.zeros_like(q_both))
        r_sc[...] = jnp.zeros(r_sc.shape, F32)
        acc_sc[...] = jnp.zeros(acc_sc.shape, F32)

        def step(blk, diag):
            start = pl.multiple_of(blk * kb, kb)
            k = k_ref[pl.ds(start, kb), :]
            v = v_ref[pl.ds(start, kb), :]
            z = _dot_nt(q, k)
            sp = jnp.maximum(z, 0.0) + jnp.log(1.0 + jnp.exp(-jnp.abs(z)))
            if diag is not None:
                mask = cols + diag * kb < rows
                sp = jnp.where(mask, sp, 0.0)
            hi = sp.astype(BF16)
            lo = (sp - hi.astype(F32)).astype(BF16)
            c = _dot(jnp.concatenate([hi, lo], axis=1), tri_ref[...])
            a = jnp.exp(z - c - r_sc[...])
            if diag is not None:
                a = jnp.where(mask, a, 0.0)
            acc_sc[...] += _dot(a.astype(BF16), v)
            r_sc[...] += c[:, 0:1]

        for d in reversed(range(nd)):
            step(qi * nd + d, d)
        n_full = qi * nd

        def body(i, carry):
            step(n_full - 1 - i, None)
            return carry

        lax.fori_loop(0, n_full, body, 0)
        outs.append(acc_sc[...])
    o_ref[...] = jnp.where(lane < SB_HEAD_DIM, outs[0], outs[1]).astype(o_ref.dtype)


def _sb_attn(q, k, v, tq, kb):
    b, s, _ = q.shape
    hp = SB_HEADS // 2
    idx = jnp.arange(kb)
    tri = (idx[:, None] >= idx[None, :]).astype(BF16)
    tri2 = jnp.concatenate([tri, tri], axis=0)
    return pl.pallas_call(
        functools.partial(_sb_kernel, tq=tq, kb=kb),
        out_shape=jax.ShapeDtypeStruct((b, s, _SBW), BF16),
        grid=(b, hp, s // tq),
        in_specs=[pl.BlockSpec((None, tq, LANES), lambda bi, h, i: (bi, i, h)),
                  pl.BlockSpec((None, s, LANES), lambda bi, h, i: (bi, 0, h)),
                  pl.BlockSpec((None, s, LANES), lambda bi, h, i: (bi, 0, h)),
                  pl.BlockSpec((2 * kb, kb), lambda bi, h, i: (0, 0))],
        out_specs=pl.BlockSpec((None, tq, LANES), lambda bi, h, i: (bi, i, h)),
        scratch_shapes=[pltpu.VMEM((tq, 1), F32), pltpu.VMEM((tq, LANES), F32)],
        compiler_params=pltpu.CompilerParams(dimension_semantics=("parallel", "parallel", "parallel"),
                                             vmem_limit_bytes=VMEM_LIMIT),
        name="sb_attn",
    )(q, k, v, tri2)


def _mem_kv_kernel(mem_ref, g_ref, w_ref, kv_ref):
    mn = _rms(mem_ref[...], g_ref[...]).astype(BF16)
    kv_ref[...] = _dot(mn, w_ref[...]).astype(BF16)


def _mem_kv(mem2, g_mem, w_xkv):
    rows, _ = mem2.shape
    n = w_xkv.shape[1]
    return pl.pallas_call(
        _mem_kv_kernel,
        out_shape=jax.ShapeDtypeStruct((rows, n), BF16),
        grid=(1,),
        in_specs=[_full(mem2.shape), _full(g_mem.shape), _full(w_xkv.shape)],
        out_specs=_full((rows, n)),
        compiler_params=pltpu.CompilerParams(vmem_limit_bytes=VMEM_LIMIT),
        name="mem_kv",
    )(mem2, g_mem, w_xkv)


def _merge_kernel(x_ref, oa_ref, ob_ref, ga_ref, gb_ref, wa_ref, wb_ref, wo_ref, gx_ref, wxq_ref, kv_ref,
                  wxo_ref, y_ref):
    merged = (ga_ref[...].astype(F32) * _dot(oa_ref[...], wa_ref[...])
              + gb_ref[...].astype(F32) * _dot(ob_ref[...], wb_ref[...]))
    x1 = x_ref[...] + _dot(merged.astype(BF16), wo_ref[...])

    hx = _rms(x1, gx_ref[...]).astype(BF16)
    xw = X_HEADS * X_HEAD_DIM
    xq = (_dot(hx, wxq_ref[...]) * (LOG2E / math.sqrt(X_HEAD_DIM))).astype(BF16)
    heads = []
    for hd in range(X_HEADS):
        sl = slice(hd * X_HEAD_DIM, (hd + 1) * X_HEAD_DIM)
        kh = kv_ref[:, sl]
        vh = kv_ref[:, xw + hd * X_HEAD_DIM: xw + (hd + 1) * X_HEAD_DIM]
        s = _dot_nt(xq[:, sl], kh)
        p = jnp.exp2(s - jnp.max(s, axis=-1, keepdims=True))
        l = jnp.sum(p, axis=-1, keepdims=True)
        heads.append((_dot(p.astype(BF16), vh) / l).astype(BF16))
    xo = jnp.concatenate(heads, axis=1)
    y_ref[...] = x1 + _dot(xo, wxo_ref[...])


def _merge(x2, oa, ob, ga, gb, wa, wb, wo, g_x, wxq, kv, wxo, tm, seq, m_len):
    t, d = x2.shape
    per_b = seq // tm
    row = lambda w: pl.BlockSpec((tm, w), lambda i: (i, 0))
    return pl.pallas_call(
        _merge_kernel,
        out_shape=jax.ShapeDtypeStruct((t, d), F32),
        grid=(t // tm,),
        in_specs=[row(d), row(oa.shape[1]), row(ob.shape[1]), row(d), row(d),
                  _full(wa.shape), _full(wb.shape), _full(wo.shape), _full(g_x.shape), _full(wxq.shape),
                  pl.BlockSpec((m_len, kv.shape[1]), lambda i: (i // per_b, 0)),
                  _full(wxo.shape)],
        out_specs=row(d),
        compiler_params=pltpu.CompilerParams(dimension_semantics=("parallel",), vmem_limit_bytes=VMEM_LIMIT),
        name="merge_xattn",
    )(x2, oa, ob, ga, gb, wa, wb, wo, g_x, wxq, kv, wxo)


def _ffn_kernel(x_ref, g_ref, wg_ref, wu_ref, wd_ref, gf_ref, y_ref, *, n_chunks):
    x = x_ref[...]
    hf = _rms(x, g_ref[...]).astype(BF16)
    d_ff = wg_ref.shape[1]
    cw = d_ff // n_chunks
    y = x
    for c in range(n_chunks):
        sl = slice(c * cw, (c + 1) * cw)
        g = _dot(hf, wg_ref[:, sl])
        u = _dot(hf, wu_ref[:, sl])
        act = (g * jax.nn.sigmoid(g) * u).astype(BF16)
        y = y + _dot(act, wd_ref[sl, :])
    y_ref[...] = _rms(y, gf_ref[...])


def _ffn(x2, g_ffn, wg, wu, wd, g_final, tm, n_chunks):
    t, d = x2.shape
    row = pl.BlockSpec((tm, d), lambda i: (i, 0))
    return pl.pallas_call(
        functools.partial(_ffn_kernel, n_chunks=n_chunks),
        out_shape=jax.ShapeDtypeStruct((t, d), F32),
        grid=(t // tm,),
        in_specs=[row, _full(g_ffn.shape), _full(wg.shape), _full(wu.shape), _full(wd.shape), _full(g_final.shape)],
        out_specs=row,
        compiler_params=pltpu.CompilerParams(dimension_semantics=("parallel",), vmem_limit_bytes=VMEM_LIMIT),
        name="ffn",
    )(x2, g_ffn, wg, wu, wd, g_final)


def _tiles(seq):
    tm = min(512, seq)
    tq = min(512, seq)
    return tm, tq


def kernel(x, mem, positions, g_mix, w_in, b_gate, g_q_lat, w_uq, g_kv_lat, w_ukv, w_a_proj, w_b_proj, w_o,
           g_x, g_mem, w_xq, w_xkv, w_xo, g_ffn, w_gate, w_up, w_down, g_final):
    b, s, d = x.shape
    m_len = mem.shape[1]
    depth = g_mix.shape[0]
    tm, tq = _tiles(s)
    t = b * s

    half = MLA_ROPE // 2
    inv_freq = ROPE_THETA ** (-jnp.arange(0, MLA_ROPE, 2, dtype=F32) / MLA_ROPE)
    invf = jnp.concatenate([jnp.zeros((MLA_NOPE,), F32), inv_freq, inv_freq,
                            jnp.zeros((LANES - MLA_NOPE - 2 * half,), F32)])[None, :]
    pos2 = positions.reshape(t, 1)
    x2 = x.reshape(t, d)
    mem2 = mem.reshape(b * m_len, d)

    for l in range(depth):
        w_packed, wuq1, wuq2, wk, wv = _pack_in_weights(w_in[l], w_uq[l], w_ukv[l], d)
        q, k, v, sbq, sbk, sbv, ga, gb = _in_proj(
            x2, pos2, invf, g_mix[l][None, :], w_packed, b_gate[l], g_q_lat[l][None, :], wuq1, wuq2,
            g_kv_lat[l][None, :], wk, wv, tm)
        o_a = _mla_attn(q.reshape(b, s, -1), k.reshape(b, s, -1), v.reshape(b, s, -1), tq, tq)
        o_b = _sb_attn(sbq.reshape(b, s, -1), sbk.reshape(b, s, -1), sbv.reshape(b, s, -1), tq, min(256, tq))
        kv = _mem_kv(mem2, g_mem[l][None, :], w_xkv[l].astype(BF16))
        x2 = _merge(x2, o_a.reshape(t, -1), o_b.reshape(t, -1), ga, gb,
                    w_a_proj[l].astype(BF16), w_b_proj[l].astype(BF16), w_o[l].astype(BF16),
                    g_x[l][None, :], w_xq[l].astype(BF16), kv, w_xo[l].astype(BF16), tm, s, m_len)
        last = l == depth - 1
        x2 = _ffn(x2, g_ffn[l][None, :], w_gate[l].astype(BF16), w_up[l].astype(BF16), w_down[l].astype(BF16),
                  g_final[None, :], tm, 2)
        assert last, "depth > 1 is not supported by the fused final norm"
    return x2.reshape(b, s, d)
```

```python
import functools
import math

import jax
import jax.numpy as jnp
from jax import lax
from jax.experimental import pallas as pl
from jax.experimental.pallas import tpu as pltpu

F32 = jnp.float32
BF16 = jnp.bfloat16

EPS = 1e-6
ROPE_THETA = 10000.0
LOG2E = math.log2(math.e)
LN2 = math.log(2.0)

MLA_HEADS = 8
MLA_Q_RANK = 256
MLA_KV_RANK = 128
MLA_NOPE = 64
MLA_ROPE = 32
MLA_V = 64
SB_HEADS = 8
SB_HEAD_DIM = 64
X_HEADS = 4
X_HEAD_DIM = 128
LANES = 128
NEG_BIG = -1e30
SOFTPLUS_LINEAR_ABOVE = 64.0

VMEM_LIMIT = 56 * 1024 * 1024


def _rms(x, g):
    ms = jnp.mean(x * x, axis=-1, keepdims=True)
    return x * lax.rsqrt(ms + EPS) * g


def _dot(a, b):
    return jnp.dot(a, b, preferred_element_type=F32)


def _dot_nt(a, b):
    return lax.dot_general(a, b, (((1,), (1,)), ((), ())), preferred_element_type=F32)


def _pipelined(n_items, stages, lags):
    offs = [0]
    for lag in lags:
        offs.append(offs[-1] + lag)
    vals = [dict() for _ in stages]
    for tick in range(n_items + offs[-1]):
        for k, stage in enumerate(stages):
            i = tick - offs[k]
            if 0 <= i < n_items:
                prev = vals[k - 1].pop(i) if k > 0 else None
                vals[k][i] = stage(i, prev)


_C_CQ = 0
_C_CKV = _C_CQ + MLA_Q_RANK
_C_KR = _C_CKV + MLA_KV_RANK
_C_KRR = _C_KR + LANES
_C_SBQ = _C_KRR + LANES
_SBW = SB_HEADS * SB_HEAD_DIM
_C_SBK = _C_SBQ + _SBW
_C_SBV = _C_SBK + _SBW
_C_GA = _C_SBV + _SBW


def _in_proj_kernel(x_ref, pos_ref, invf_ref, gmix_ref, win_ref, bg_ref, gq_ref, wuq1_ref, wuq2_ref,
                    gkv_ref, wk_ref, wv_ref,
                    q_ref, k_ref, v_ref, sbq_ref, sbk_ref, sbv_ref, ga_ref, gb_ref, *, d_model):
    x = x_ref[...]
    h = _rms(x, gmix_ref[...]).astype(BF16)

    ang = pos_ref[...].astype(F32) * invf_ref[...]
    cos = jnp.cos(ang)
    sin = jnp.sin(ang)

    def proj(c0, width):
        return _dot(h, win_ref[:, c0:c0 + width])

    cq = _rms(proj(_C_CQ, MLA_Q_RANK), gq_ref[...]).astype(BF16)
    q1 = _dot(cq, wuq1_ref[...])
    q2 = _dot(cq, wuq2_ref[...])
    q_scale = LOG2E / math.sqrt(MLA_NOPE + MLA_ROPE)
    for hd in range(MLA_HEADS):
        sl = slice(hd * LANES, (hd + 1) * LANES)
        q_ref[:, sl] = ((q1[:, sl] * cos + q2[:, sl] * sin) * q_scale).astype(BF16)

    ckv = _rms(proj(_C_CKV, MLA_KV_RANK), gkv_ref[...]).astype(BF16)
    kn = _dot(ckv, wk_ref[...])
    v_lane = lax.broadcasted_iota(jnp.int32, (1, MLA_HEADS * LANES), 1)
    ones_cols = jnp.where(v_lane % LANES >= MLA_V, 1.0, 0.0).astype(F32)
    v_ref[...] = (_dot(ckv, wv_ref[...]) + ones_cols).astype(BF16)
    krope = proj(_C_KR, LANES) * cos + proj(_C_KRR, LANES) * sin
    for hd in range(MLA_HEADS):
        sl = slice(hd * LANES, (hd + 1) * LANES)
        k_ref[:, sl] = (kn[:, sl] + krope).astype(BF16)

    sbq_ref[...] = (proj(_C_SBQ, _SBW) * (LOG2E / math.sqrt(SB_HEAD_DIM))).astype(BF16)
    sbk_ref[...] = proj(_C_SBK, _SBW).astype(BF16)
    sbv_ref[...] = proj(_C_SBV, _SBW).astype(BF16)

    ga_ref[...] = jax.nn.sigmoid(proj(_C_GA, d_model) + bg_ref[0:1, :]).astype(BF16)
    gb_ref[...] = jax.nn.sigmoid(proj(_C_GA + d_model, d_model) + bg_ref[1:2, :]).astype(BF16)


def _pack_in_weights(w_in, w_uq, w_ukv, d_model):
    f = w_in.dtype
    zeros = lambda n: jnp.zeros((d_model, n), f)
    o_kr = MLA_Q_RANK + MLA_KV_RANK
    o_sb = o_kr + MLA_ROPE
    half = MLA_ROPE // 2
    kr = w_in[:, o_kr:o_kr + MLA_ROPE]
    kr_rot = jnp.concatenate([-kr[:, half:], kr[:, :half]], axis=1)
    pad_tail = LANES - MLA_NOPE - MLA_ROPE
    w_packed = jnp.concatenate([
        w_in[:, :o_kr],
        zeros(MLA_NOPE), kr, zeros(pad_tail),
        zeros(MLA_NOPE), kr_rot, zeros(pad_tail),
        w_in[:, o_sb:],
    ], axis=1).astype(BF16)

    r = w_uq.shape[0]
    wq = w_uq.reshape(r, MLA_HEADS, MLA_NOPE + MLA_ROPE)
    rope = wq[:, :, MLA_NOPE:]
    rope_rot = jnp.concatenate([-rope[:, :, half:], rope[:, :, :half]], axis=2)
    zq = jnp.zeros((r, MLA_HEADS, pad_tail), f)
    wuq1 = jnp.concatenate([wq, zq], axis=2).reshape(r, MLA_HEADS * LANES).astype(BF16)
    wuq2 = jnp.concatenate([jnp.zeros((r, MLA_HEADS, MLA_NOPE), f), rope_rot, zq], axis=2)
    wuq2 = wuq2.reshape(r, MLA_HEADS * LANES).astype(BF16)

    rk = w_ukv.shape[0]
    wkv = w_ukv.reshape(rk, MLA_HEADS, MLA_NOPE + MLA_V)
    wk = jnp.concatenate([wkv[:, :, :MLA_NOPE], jnp.zeros((rk, MLA_HEADS, LANES - MLA_NOPE), f)], axis=2)
    wk = wk.reshape(rk, MLA_HEADS * LANES).astype(BF16)
    wv = jnp.concatenate([wkv[:, :, MLA_NOPE:], jnp.zeros((rk, MLA_HEADS, LANES - MLA_V), f)], axis=2)
    wv = wv.reshape(rk, MLA_HEADS * LANES).astype(BF16)
    return w_packed, wuq1, wuq2, wk, wv


def _full(shape):
    n = len(shape)
    return pl.BlockSpec(shape, lambda *_: (0,) * n)


def _in_proj(x2, pos2, invf, g_mix, w_packed, b_gate, g_q, wuq1, wuq2, g_kv, wk, wv, tm):
    t, d = x2.shape
    row = lambda w: pl.BlockSpec((tm, w), lambda i: (i, 0))
    out_widths = [MLA_HEADS * LANES, MLA_HEADS * LANES, MLA_HEADS * LANES, _SBW, _SBW, _SBW, d, d]
    return pl.pallas_call(
        functools.partial(_in_proj_kernel, d_model=d),
        out_shape=[jax.ShapeDtypeStruct((t, w), BF16) for w in out_widths],
        grid=(t // tm,),
        in_specs=[row(d), row(1), _full(invf.shape), _full(g_mix.shape), _full(w_packed.shape),
                  _full(b_gate.shape), _full(g_q.shape), _full(wuq1.shape), _full(wuq2.shape),
                  _full(g_kv.shape), _full(wk.shape), _full(wv.shape)],
        out_specs=[row(w) for w in out_widths],
        compiler_params=pltpu.CompilerParams(dimension_semantics=("parallel",), vmem_limit_bytes=VMEM_LIMIT),
        name="in_proj",
    )(x2, pos2, invf, g_mix, w_packed, b_gate, g_q, wuq1, wuq2, g_kv, wk, wv)


def _diag_chunks(n_chunks, rc, kb, diag, strict):
    out = []
    for r in range(n_chunks):
        lo_row, hi_row = r * rc, (r + 1) * rc - 1
        lo_col, hi_col = diag * kb, (diag + 1) * kb - 1
        if (lo_col >= hi_row) if strict else (lo_col > hi_row):
            continue
        fully_visible = (hi_col < lo_row) if strict else (hi_col <= lo_row)
        out.append((r, not fully_visible))
    return out


def _mla_kernel(q_ref, k_ref, v_ref, o_ref, m_sc, acc_sc, *, tq, kb, rc, look):
    qi = pl.program_id(2)
    nd = tq // kb
    nrc = tq // rc
    rows = lax.broadcasted_iota(jnp.int32, (rc, kb), 0)
    cols = lax.broadcasted_iota(jnp.int32, (rc, kb), 1)
    m_sc[...] = jnp.full(m_sc.shape, NEG_BIG, F32)
    acc_sc[...] = jnp.zeros(acc_sc.shape, F32)

    def step(blk, diag):
        start = pl.multiple_of(blk * kb, kb)
        if diag is None:
            work = [(j, r, False) for j in range(2) for r in range(nrc)]
        else:
            work = [(j, r, mk) for j in range(2) for (r, mk) in _diag_chunks(nrc, rc, kb, diag, strict=False)]

        def scores(i, _):
            j, r, _mk = work[i]
            hs = slice(j * LANES, (j + 1) * LANES)
            return _dot_nt(q_ref[r * rc:(r + 1) * rc, hs], k_ref[pl.ds(start, kb), hs])

        def finish(i, s):
            j, r, mk = work[i]
            hs = slice(j * LANES, (j + 1) * LANES)
            rs = slice(r * rc, (r + 1) * rc)
            if mk:
                s = jnp.where(cols + (diag * kb - r * rc) <= rows, s, NEG_BIG)
            m_prev = m_sc[j, rs, :]
            m_new = jnp.maximum(m_prev, jnp.max(s, axis=-1, keepdims=True))
            alpha = jnp.exp2(m_prev - m_new)
            p = jnp.exp2(s - jnp.concatenate([m_new] * (kb // LANES), axis=1))
            pv = _dot(p.astype(BF16), v_ref[pl.ds(start, kb), hs])
            acc_sc[j, rs, :] = alpha * acc_sc[j, rs, :] + pv
            m_sc[j, rs, :] = m_new

        _pipelined(len(work), [scores, finish], [look])

    def body(blk, carry):
        step(blk, None)
        return carry

    lax.fori_loop(0, qi * nd, body, 0)
    for d in range(nd):
        step(qi * nd + d, d)

    lane = lax.broadcasted_iota(jnp.int32, (tq, LANES), 1)
    a0 = acc_sc[0]
    a1 = acc_sc[1]
    o0 = a0 / pltpu.roll(a0, MLA_V, axis=1)
    o1 = pltpu.roll(a1, MLA_V, axis=1) / a1
    o_ref[...] = jnp.where(lane < MLA_V, o0, o1).astype(o_ref.dtype)


def _mla_attn(q, k, v, tq, kb, rc, look):
    b, s, _ = q.shape
    hp = MLA_HEADS // 2
    return pl.pallas_call(
        functools.partial(_mla_kernel, tq=tq, kb=kb, rc=rc, look=look),
        out_shape=jax.ShapeDtypeStruct((b, s, MLA_HEADS * MLA_V), BF16),
        grid=(b, hp, s // tq),
        in_specs=[pl.BlockSpec((None, tq, 2 * LANES), lambda bi, h, i: (bi, i, h)),
                  pl.BlockSpec((None, s, 2 * LANES), lambda bi, h, i: (bi, 0, h)),
                  pl.BlockSpec((None, s, 2 * LANES), lambda bi, h, i: (bi, 0, h))],
        out_specs=pl.BlockSpec((None, tq, LANES), lambda bi, h, i: (bi, i, h)),
        scratch_shapes=[pltpu.VMEM((2, tq, LANES), F32), pltpu.VMEM((2, tq, LANES), F32)],
        compiler_params=pltpu.CompilerParams(dimension_semantics=("parallel", "parallel", "parallel"),
                                             vmem_limit_bytes=VMEM_LIMIT),
        name="mla_attn",
    )(q, k, v)


def _sb_kernel(q_ref, k_ref, v_ref, tri_ref, o_ref, qz_sc, r_sc, acc_sc, *, tq, kb, rc, lag_a, lag_b):
    qi = pl.program_id(2)
    nd = tq // kb
    nrc = tq // rc
    rows = lax.broadcasted_iota(jnp.int32, (rc, kb), 0)
    cols = lax.broadcasted_iota(jnp.int32, (rc, kb), 1)
    lane = lax.broadcasted_iota(jnp.int32, (tq, LANES), 1)
    q_both = q_ref[...]
    for j in range(2):
        in_head = (lane >= j * SB_HEAD_DIM) & (lane < (j + 1) * SB_HEAD_DIM)
        qz_sc[j] = jnp.where(in_head, q_both, jnp.zeros_like(q_both))
    r_sc[...] = jnp.zeros(r_sc.shape, F32)
    acc_sc[...] = jnp.zeros(acc_sc.shape, F32)

    def step(blk, diag):
        start = pl.multiple_of(blk * kb, kb)
        if diag is None:
            work = [(j, r, False) for j in range(2) for r in range(nrc)]
        else:
            work = [(j, r, mk) for j in range(2) for (r, mk) in _diag_chunks(nrc, rc, kb, diag, strict=True)]

        def mask_of(i):
            _j, r, _mk = work[i]
            return cols + (diag * kb - r * rc) < rows

        def scores(i, _):
            j, r, _mk = work[i]
            return _dot_nt(qz_sc[j, r * rc:(r + 1) * rc, :], k_ref[pl.ds(start, kb), :])

        def suffix(i, z):
            j, r, mk = work[i]
            rs = slice(r * rc, (r + 1) * rc)
            sp = jnp.where(z > SOFTPLUS_LINEAR_ABOVE, z, jnp.log(1.0 + jnp.exp2(z)) * LOG2E)
            if mk:
                sp = jnp.where(mask_of(i), sp, 0.0)
            hi = sp.astype(BF16)
            lo = (sp - hi.astype(F32)).astype(BF16)
            c = _dot(jnp.concatenate([hi, lo], axis=1), tri_ref[...])
            r_prev = r_sc[j, rs, :]
            r_sc[j, rs, :] = r_prev + jnp.sum(sp, axis=-1, keepdims=True)
            return z, c, r_prev

        def weights(i, zcr):
            j, r, mk = work[i]
            rs = slice(r * rc, (r + 1) * rc)
            z, c, r_prev = zcr
            a = jnp.exp2(z - c - jnp.concatenate([r_prev] * (kb // LANES), axis=1))
            if mk:
                a = jnp.where(mask_of(i), a, 0.0)
            acc_sc[j, rs, :] += _dot(a.astype(BF16), v_ref[pl.ds(start, kb), :])

        _pipelined(len(work), [scores, suffix, weights], [lag_a, lag_b])

    for d in reversed(range(nd)):
        step(qi * nd + d, d)
    n_full = qi * nd

    def body(i, carry):
        step(n_full - 1 - i, None)
        return carry

    lax.fori_loop(0, n_full, body, 0)
    o_ref[...] = jnp.where(lane < SB_HEAD_DIM, acc_sc[0], acc_sc[1]).astype(o_ref.dtype)


def _sb_attn(q, k, v, tq, kb, rc, lag_a, lag_b):
    b, s, _ = q.shape
    hp = SB_HEADS // 2
    idx = jnp.arange(kb)
    tri = (idx[:, None] >= idx[None, :]).astype(BF16)
    tri2 = jnp.concatenate([tri, tri], axis=0)
    return pl.pallas_call(
        functools.partial(_sb_kernel, tq=tq, kb=kb, rc=rc, lag_a=lag_a, lag_b=lag_b),
        out_shape=jax.ShapeDtypeStruct((b, s, _SBW), BF16),
        grid=(b, hp, s // tq),
        in_specs=[pl.BlockSpec((None, tq, LANES), lambda bi, h, i: (bi, i, h)),
                  pl.BlockSpec((None, s, LANES), lambda bi, h, i: (bi, 0, h)),
                  pl.BlockSpec((None, s, LANES), lambda bi, h, i: (bi, 0, h)),
                  pl.BlockSpec((2 * kb, kb), lambda bi, h, i: (0, 0))],
        out_specs=pl.BlockSpec((None, tq, LANES), lambda bi, h, i: (bi, i, h)),
        scratch_shapes=[pltpu.VMEM((2, tq, LANES), BF16), pltpu.VMEM((2, tq, LANES), F32),
                        pltpu.VMEM((2, tq, LANES), F32)],
        compiler_params=pltpu.CompilerParams(dimension_semantics=("parallel", "parallel", "parallel"),
                                             vmem_limit_bytes=VMEM_LIMIT),
        name="sb_attn",
    )(q, k, v, tri2)


def _mem_kv_kernel(mem_ref, g_ref, w_ref, kv_ref):
    mn = _rms(mem_ref[...], g_ref[...]).astype(BF16)
    kv_ref[...] = _dot(mn, w_ref[...]).astype(BF16)


def _mem_kv(mem2, g_mem, w_xkv):
    rows, _ = mem2.shape
    n = w_xkv.shape[1]
    return pl.pallas_call(
        _mem_kv_kernel,
        out_shape=jax.ShapeDtypeStruct((rows, n), BF16),
        grid=(1,),
        in_specs=[_full(mem2.shape), _full(g_mem.shape), _full(w_xkv.shape)],
        out_specs=_full((rows, n)),
        compiler_params=pltpu.CompilerParams(vmem_limit_bytes=VMEM_LIMIT),
        name="mem_kv",
    )(mem2, g_mem, w_xkv)


def _merge_kernel(x_ref, oa_ref, ob_ref, ga_ref, gb_ref, wa_ref, wb_ref, wo_ref, gx_ref, wxq_ref, kv_ref,
                  wxo_ref, y_ref):
    merged = (ga_ref[...].astype(F32) * _dot(oa_ref[...], wa_ref[...])
              + gb_ref[...].astype(F32) * _dot(ob_ref[...], wb_ref[...]))
    x1 = x_ref[...] + _dot(merged.astype(BF16), wo_ref[...])

    hx = _rms(x1, gx_ref[...]).astype(BF16)
    xw = X_HEADS * X_HEAD_DIM
    xq = (_dot(hx, wxq_ref[...]) * (LOG2E / math.sqrt(X_HEAD_DIM))).astype(BF16)
    heads = []
    for hd in range(X_HEADS):
        sl = slice(hd * X_HEAD_DIM, (hd + 1) * X_HEAD_DIM)
        kh = kv_ref[:, sl]
        vh = kv_ref[:, xw + hd * X_HEAD_DIM: xw + (hd + 1) * X_HEAD_DIM]
        s = _dot_nt(xq[:, sl], kh)
        p = jnp.exp2(s - jnp.max(s, axis=-1, keepdims=True))
        l = jnp.sum(p, axis=-1, keepdims=True)
        heads.append((_dot(p.astype(BF16), vh) / l).astype(BF16))
    xo = jnp.concatenate(heads, axis=1)
    y_ref[...] = x1 + _dot(xo, wxo_ref[...])


def _merge(x2, oa, ob, ga, gb, wa, wb, wo, g_x, wxq, kv, wxo, tm, seq, m_len):
    t, d = x2.shape
    per_b = seq // tm
    row = lambda w: pl.BlockSpec((tm, w), lambda i: (i, 0))
    return pl.pallas_call(
        _merge_kernel,
        out_shape=jax.ShapeDtypeStruct((t, d), F32),
        grid=(t // tm,),
        in_specs=[row(d), row(oa.shape[1]), row(ob.shape[1]), row(d), row(d),
                  _full(wa.shape), _full(wb.shape), _full(wo.shape), _full(g_x.shape), _full(wxq.shape),
                  pl.BlockSpec((m_len, kv.shape[1]), lambda i: (i // per_b, 0)),
                  _full(wxo.shape)],
        out_specs=row(d),
        compiler_params=pltpu.CompilerParams(dimension_semantics=("parallel",), vmem_limit_bytes=VMEM_LIMIT),
        name="merge_xattn",
    )(x2, oa, ob, ga, gb, wa, wb, wo, g_x, wxq, kv, wxo)


def _ffn_kernel(x_ref, g_ref, wg_ref, wu_ref, wd_ref, gf_ref, y_ref, *, n_chunks):
    x = x_ref[...]
    hf = _rms(x, g_ref[...]).astype(BF16)
    d_ff = wg_ref.shape[1]
    cw = d_ff // n_chunks
    y = x
    for c in range(n_chunks):
        sl = slice(c * cw, (c + 1) * cw)
        g = _dot(hf, wg_ref[:, sl])
        u = _dot(hf, wu_ref[:, sl])
        act = (g * jax.nn.sigmoid(g) * u).astype(BF16)
        y = y + _dot(act, wd_ref[sl, :])
    y_ref[...] = _rms(y, gf_ref[...])


def _ffn(x2, g_ffn, wg, wu, wd, g_final, tm, n_chunks):
    t, d = x2.shape
    row = pl.BlockSpec((tm, d), lambda i: (i, 0))
    return pl.pallas_call(
        functools.partial(_ffn_kernel, n_chunks=n_chunks),
        out_shape=jax.ShapeDtypeStruct((t, d), F32),
        grid=(t // tm,),
        in_specs=[row, _full(g_ffn.shape), _full(wg.shape), _full(wu.shape), _full(wd.shape), _full(g_final.shape)],
        out_specs=row,
        compiler_params=pltpu.CompilerParams(dimension_semantics=("parallel",), vmem_limit_bytes=VMEM_LIMIT),
        name="ffn",
    )(x2, g_ffn, wg, wu, wd, g_final)


def kernel(x, mem, positions, g_mix, w_in, b_gate, g_q_lat, w_uq, g_kv_lat, w_ukv, w_a_proj, w_b_proj, w_o,
           g_x, g_mem, w_xq, w_xkv, w_xo, g_ffn, w_gate, w_up, w_down, g_final):
    b, s, d = x.shape
    m_len = mem.shape[1]
    assert g_mix.shape[0] == 1, "the final RMSNorm is fused into the (single) layer's FFN kernel"
    tm = min(512, s)
    tq = min(1024, s)
    kb = min(256, s)
    rc = min(128, s)
    t = b * s

    half = MLA_ROPE // 2
    inv_freq = ROPE_THETA ** (-jnp.arange(0, MLA_ROPE, 2, dtype=F32) / MLA_ROPE)
    invf = jnp.concatenate([jnp.zeros((MLA_NOPE,), F32), inv_freq, inv_freq,
                            jnp.zeros((LANES - MLA_NOPE - 2 * half,), F32)])[None, :]
    pos2 = positions.reshape(t, 1)
    x2 = x.reshape(t, d)
    mem2 = mem.reshape(b * m_len, d)

    w_packed, wuq1, wuq2, wk, wv = _pack_in_weights(w_in[0], w_uq[0], w_ukv[0], d)
    q, k, v, sbq, sbk, sbv, ga, gb = _in_proj(
        x2, pos2, invf, g_mix[0][None, :], w_packed, b_gate[0], g_q_lat[0][None, :], wuq1, wuq2,
        g_kv_lat[0][None, :], wk, wv, tm)
    o_a = _mla_attn(q.reshape(b, s, -1), k.reshape(b, s, -1), v.reshape(b, s, -1), tq, kb, rc, look=3)
    o_b = _sb_attn(sbq.reshape(b, s, -1), sbk.reshape(b, s, -1), sbv.reshape(b, s, -1), tq, kb, rc,
                   lag_a=3, lag_b=3)
    kv = _mem_kv(mem2, g_mem[0][None, :], w_xkv[0].astype(BF16))
    x2 = _merge(x2, o_a.reshape(t, -1), o_b.reshape(t, -1), ga, gb,
                w_a_proj[0].astype(BF16), w_b_proj[0].astype(BF16), w_o[0].astype(BF16),
                g_x[0][None, :], w_xq[0].astype(BF16), kv, w_xo[0].astype(BF16), tm, s, m_len)
    x2 = _ffn(x2, g_ffn[0][None, :], w_gate[0].astype(BF16), w_up[0].astype(BF16), w_down[0].astype(BF16),
              g_final[None, :], tm, 2)
    return x2.reshape(b, s, d)
```

```python
import functools
import math

import jax
import jax.numpy as jnp
from jax import lax
from jax.experimental import pallas as pl
from jax.experimental.pallas import tpu as pltpu

F32 = jnp.float32
BF16 = jnp.bfloat16

EPS = 1e-6
ROPE_THETA = 10000.0
LOG2E = math.log2(math.e)
LN2 = math.log(2.0)

MLA_HEADS = 8
MLA_Q_RANK = 256
MLA_KV_RANK = 128
MLA_NOPE = 64
MLA_ROPE = 32
MLA_V = 64
SB_HEADS = 8
SB_HEAD_DIM = 64
X_HEADS = 4
X_HEAD_DIM = 128
LANES = 128
SUBLANES = 8
NEG_BIG = -1e30
SB_UNDERFLOW_LOG2 = 160.0
SOFTPLUS_LINEAR_ABOVE = 64.0

VMEM_LIMIT = 56 * 1024 * 1024


def _rms(x, g):
    ms = jnp.mean(x * x, axis=-1, keepdims=True)
    return x * lax.rsqrt(ms + EPS) * g


def _dot(a, b):
    return jnp.dot(a, b, preferred_element_type=F32)


def _dot_nt(a, b):
    return lax.dot_general(a, b, (((1,), (1,)), ((), ())), preferred_element_type=F32)


def _pipelined(n_items, stages, lags):
    offs = [0]
    for lag in lags:
        offs.append(offs[-1] + lag)
    vals = [dict() for _ in stages]
    for tick in range(n_items + offs[-1]):
        for k, stage in enumerate(stages):
            i = tick - offs[k]
            if 0 <= i < n_items:
                prev = vals[k - 1].pop(i) if k > 0 else None
                vals[k][i] = stage(i, prev)


_C_CQ = 0
_C_CKV = _C_CQ + MLA_Q_RANK
_C_KR = _C_CKV + MLA_KV_RANK
_C_KRR = _C_KR + LANES
_C_SBQ = _C_KRR + LANES
_SBW = SB_HEADS * SB_HEAD_DIM
_C_SBK = _C_SBQ + _SBW
_C_SBV = _C_SBK + _SBW
_C_GA = _C_SBV + _SBW


def _in_proj_kernel(x_ref, pos_ref, invf_ref, gmix_ref, win_ref, bg_ref, gq_ref, wuq1_ref, wuq2_ref,
                    gkv_ref, wk_ref, wv_ref,
                    q_ref, k_ref, v_ref, sbq_ref, sbk_ref, sbv_ref, ga_ref, gb_ref, *, d_model):
    x = x_ref[...]
    h = _rms(x, gmix_ref[...]).astype(BF16)

    ang = pos_ref[...].astype(F32) * invf_ref[...]
    cos = jnp.cos(ang)
    sin = jnp.sin(ang)

    def proj(c0, width):
        return _dot(h, win_ref[:, c0:c0 + width])

    cq = _rms(proj(_C_CQ, MLA_Q_RANK), gq_ref[...]).astype(BF16)
    q1 = _dot(cq, wuq1_ref[...])
    q2 = _dot(cq, wuq2_ref[...])
    q_scale = LOG2E / math.sqrt(MLA_NOPE + MLA_ROPE)
    for hd in range(MLA_HEADS):
        sl = slice(hd * LANES, (hd + 1) * LANES)
        q_ref[:, sl] = ((q1[:, sl] * cos + q2[:, sl] * sin) * q_scale).astype(BF16)

    ckv = _rms(proj(_C_CKV, MLA_KV_RANK), gkv_ref[...]).astype(BF16)
    kn = _dot(ckv, wk_ref[...])
    v_lane = lax.broadcasted_iota(jnp.int32, (1, MLA_HEADS * LANES), 1)
    ones_cols = jnp.where(v_lane % LANES >= MLA_V, 1.0, 0.0).astype(F32)
    v_ref[...] = (_dot(ckv, wv_ref[...]) + ones_cols).astype(BF16)
    krope = proj(_C_KR, LANES) * cos + proj(_C_KRR, LANES) * sin
    for hd in range(MLA_HEADS):
        sl = slice(hd * LANES, (hd + 1) * LANES)
        k_ref[:, sl] = (kn[:, sl] + krope).astype(BF16)

    sbq_ref[...] = (proj(_C_SBQ, _SBW) * (LOG2E / math.sqrt(SB_HEAD_DIM))).astype(BF16)
    sbk_ref[...] = proj(_C_SBK, _SBW).astype(BF16)
    sbv_ref[...] = proj(_C_SBV, _SBW).astype(BF16)

    ga_ref[...] = jax.nn.sigmoid(proj(_C_GA, d_model) + bg_ref[0:1, :]).astype(BF16)
    gb_ref[...] = jax.nn.sigmoid(proj(_C_GA + d_model, d_model) + bg_ref[1:2, :]).astype(BF16)


def _pack_in_weights(w_in, w_uq, w_ukv, d_model):
    f = w_in.dtype
    zeros = lambda n: jnp.zeros((d_model, n), f)
    o_kr = MLA_Q_RANK + MLA_KV_RANK
    o_sb = o_kr + MLA_ROPE
    half = MLA_ROPE // 2
    kr = w_in[:, o_kr:o_kr + MLA_ROPE]
    kr_rot = jnp.concatenate([-kr[:, half:], kr[:, :half]], axis=1)
    pad_tail = LANES - MLA_NOPE - MLA_ROPE
    w_packed = jnp.concatenate([
        w_in[:, :o_kr],
        zeros(MLA_NOPE), kr, zeros(pad_tail),
        zeros(MLA_NOPE), kr_rot, zeros(pad_tail),
        w_in[:, o_sb:],
    ], axis=1).astype(BF16)

    r = w_uq.shape[0]
    wq = w_uq.reshape(r, MLA_HEADS, MLA_NOPE + MLA_ROPE)
    rope = wq[:, :, MLA_NOPE:]
    rope_rot = jnp.concatenate([-rope[:, :, half:], rope[:, :, :half]], axis=2)
    zq = jnp.zeros((r, MLA_HEADS, pad_tail), f)
    wuq1 = jnp.concatenate([wq, zq], axis=2).reshape(r, MLA_HEADS * LANES).astype(BF16)
    wuq2 = jnp.concatenate([jnp.zeros((r, MLA_HEADS, MLA_NOPE), f), rope_rot, zq], axis=2)
    wuq2 = wuq2.reshape(r, MLA_HEADS * LANES).astype(BF16)

    rk = w_ukv.shape[0]
    wkv = w_ukv.reshape(rk, MLA_HEADS, MLA_NOPE + MLA_V)
    wk = jnp.concatenate([wkv[:, :, :MLA_NOPE], jnp.zeros((rk, MLA_HEADS, LANES - MLA_NOPE), f)], axis=2)
    wk = wk.reshape(rk, MLA_HEADS * LANES).astype(BF16)
    wv = jnp.concatenate([wkv[:, :, MLA_NOPE:], jnp.zeros((rk, MLA_HEADS, LANES - MLA_V), f)], axis=2)
    wv = wv.reshape(rk, MLA_HEADS * LANES).astype(BF16)
    return w_packed, wuq1, wuq2, wk, wv


def _full(shape):
    n = len(shape)
    return pl.BlockSpec(shape, lambda *_: (0,) * n)


def _in_proj(x2, pos2, invf, g_mix, w_packed, b_gate, g_q, wuq1, wuq2, g_kv, wk, wv, tm):
    t, d = x2.shape
    row = lambda w: pl.BlockSpec((tm, w), lambda i: (i, 0))
    out_widths = [MLA_HEADS * LANES, MLA_HEADS * LANES, MLA_HEADS * LANES, _SBW, _SBW, _SBW, d, d]
    return pl.pallas_call(
        functools.partial(_in_proj_kernel, d_model=d),
        out_shape=[jax.ShapeDtypeStruct((t, w), BF16) for w in out_widths],
        grid=(t // tm,),
        in_specs=[row(d), row(1), _full(invf.shape), _full(g_mix.shape), _full(w_packed.shape),
                  _full(b_gate.shape), _full(g_q.shape), _full(wuq1.shape), _full(wuq2.shape),
                  _full(g_kv.shape), _full(wk.shape), _full(wv.shape)],
        out_specs=[row(w) for w in out_widths],
        compiler_params=pltpu.CompilerParams(dimension_semantics=("parallel",), vmem_limit_bytes=VMEM_LIMIT),
        name="in_proj",
    )(x2, pos2, invf, g_mix, w_packed, b_gate, g_q, wuq1, wuq2, g_kv, wk, wv)


def _work_items(blocks, n_chunks, rc, kb, strict):
    items = []
    for boff, diag in blocks:
        for j in range(2):
            for r in range(n_chunks):
                if diag is None:
                    items.append((boff, diag, j, r, False))
                    continue
                lo_row, hi_row = r * rc, (r + 1) * rc - 1
                lo_col, hi_col = diag * kb, (diag + 1) * kb - 1
                if (lo_col >= hi_row) if strict else (lo_col > hi_row):
                    continue
                fully_visible = (hi_col < lo_row) if strict else (hi_col <= lo_row)
                items.append((boff, diag, j, r, not fully_visible))
    return items


def _mla_kernel(q_ref, k_ref, v_ref, o_ref, m_sc, acc_sc, *, tq, kb, rc, look, blocks_per_body):
    qi = pl.program_id(2)
    nd = tq // kb
    nrc = tq // rc
    rows = lax.broadcasted_iota(jnp.int32, (rc, kb), 0)
    cols = lax.broadcasted_iota(jnp.int32, (rc, kb), 1)
    m_sc[...] = jnp.full(m_sc.shape, NEG_BIG, F32)
    acc_sc[...] = jnp.zeros(acc_sc.shape, F32)

    def region(base_blk, work):
        def kv_rows(i):
            return pl.ds(pl.multiple_of((base_blk + work[i][0]) * kb, kb), kb)

        def scores(i, _):
            _boff, _diag, j, r, _mk = work[i]
            hs = slice(j * LANES, (j + 1) * LANES)
            return _dot_nt(q_ref[r * rc:(r + 1) * rc, hs], k_ref[kv_rows(i), hs])

        def finish(i, s):
            _boff, diag, j, r, mk = work[i]
            hs = slice(j * LANES, (j + 1) * LANES)
            rs = slice(r * rc, (r + 1) * rc)
            if mk:
                s = jnp.where(cols + (diag * kb - r * rc) <= rows, s, NEG_BIG)
            m_prev = m_sc[j, rs, :]
            m_new = jnp.maximum(m_prev, jnp.max(s, axis=-1, keepdims=True))
            alpha = jnp.exp2(m_prev - m_new)
            p = jnp.exp2(s - jnp.concatenate([m_new] * (kb // LANES), axis=1))
            pv = _dot(p.astype(BF16), v_ref[kv_rows(i), hs])
            acc_sc[j, rs, :] = alpha * acc_sc[j, rs, :] + pv
            m_sc[j, rs, :] = m_new

        _pipelined(len(work), [scores, finish], [look])

    full_work = _work_items([(o, None) for o in range(blocks_per_body)], nrc, rc, kb, strict=False)

    def body(i, carry):
        region(i * blocks_per_body, full_work)
        return carry

    lax.fori_loop(0, qi * (nd // blocks_per_body), body, 0)
    for d in range(nd):
        region(qi * nd, _work_items([(d, d)], nrc, rc, kb, strict=False))

    lane = lax.broadcasted_iota(jnp.int32, (tq, LANES), 1)
    a0 = acc_sc[0]
    a1 = acc_sc[1]
    o0 = a0 / pltpu.roll(a0, MLA_V, axis=1)
    o1 = pltpu.roll(a1, MLA_V, axis=1) / a1
    o_ref[...] = jnp.where(lane < MLA_V, o0, o1).astype(o_ref.dtype)


def _mla_attn(q, k, v, tq, kb, rc, look, blocks_per_body):
    b, s, _ = q.shape
    hp = MLA_HEADS // 2
    assert (tq // kb) % blocks_per_body == 0
    return pl.pallas_call(
        functools.partial(_mla_kernel, tq=tq, kb=kb, rc=rc, look=look, blocks_per_body=blocks_per_body),
        out_shape=jax.ShapeDtypeStruct((b, s, MLA_HEADS * MLA_V), BF16),
        grid=(b, hp, s // tq),
        in_specs=[pl.BlockSpec((None, tq, 2 * LANES), lambda bi, h, i: (bi, i, h)),
                  pl.BlockSpec((None, s, 2 * LANES), lambda bi, h, i: (bi, 0, h)),
                  pl.BlockSpec((None, s, 2 * LANES), lambda bi, h, i: (bi, 0, h))],
        out_specs=pl.BlockSpec((None, tq, LANES), lambda bi, h, i: (bi, i, h)),
        scratch_shapes=[pltpu.VMEM((2, tq, LANES), F32), pltpu.VMEM((2, tq, LANES), F32)],
        compiler_params=pltpu.CompilerParams(dimension_semantics=("parallel", "parallel", "parallel"),
                                             vmem_limit_bytes=VMEM_LIMIT),
        name="mla_attn",
    )(q, k, v)


def _sb_kernel(q_ref, k_ref, v_ref, tri_ref, o_ref, qz_sc, r_sc, acc_sc, *, tq, kb, rc, lag_a, lag_b):
    qi = pl.program_id(2)
    nd = tq // kb
    nrc = tq // rc
    rows = lax.broadcasted_iota(jnp.int32, (rc, kb), 0)
    cols = lax.broadcasted_iota(jnp.int32, (rc, kb), 1)
    lane = lax.broadcasted_iota(jnp.int32, (tq, LANES), 1)
    q_both = q_ref[...]
    for j in range(2):
        in_head = (lane >= j * SB_HEAD_DIM) & (lane < (j + 1) * SB_HEAD_DIM)
        qz_sc[j] = jnp.where(in_head, q_both, jnp.zeros_like(q_both))
    r_sc[...] = jnp.zeros(r_sc.shape, F32)
    acc_sc[...] = jnp.zeros(acc_sc.shape, F32)

    def region(base_blk, work):
        r_min = [None]

        def kv_rows(i):
            return pl.ds(pl.multiple_of((base_blk + work[i][0]) * kb, kb), kb)

        def mask_of(i):
            _boff, diag, _j, r, _mk = work[i]
            return cols + (diag * kb - r * rc) < rows

        def scores(i, _):
            _boff, _diag, j, r, _mk = work[i]
            return _dot_nt(qz_sc[j, r * rc:(r + 1) * rc, :], k_ref[kv_rows(i), :])

        def suffix(i, z):
            _boff, _diag, j, r, mk = work[i]
            rs = slice(r * rc, (r + 1) * rc)
            sp = jnp.where(z > SOFTPLUS_LINEAR_ABOVE, z, jnp.log(1.0 + jnp.exp2(z)) * LOG2E)
            if mk:
                sp = jnp.where(mask_of(i), sp, 0.0)
            hi = sp.astype(BF16)
            lo = (sp - hi.astype(F32)).astype(BF16)
            c = _dot(jnp.concatenate([hi, lo], axis=1), tri_ref[...])
            r_prev = r_sc[j, rs, :]
            r_new = r_prev + jnp.sum(sp, axis=-1, keepdims=True)
            r_sc[j, rs, :] = r_new
            parts = [r_new[t8 * SUBLANES:(t8 + 1) * SUBLANES, :] for t8 in range(rc // SUBLANES)]
            while len(parts) > 1:
                parts = [jnp.minimum(a, b) for a, b in zip(parts[0::2], parts[1::2])]
            r_min[0] = parts[0] if r_min[0] is None else jnp.minimum(r_min[0], parts[0])
            return z, c, r_prev

        def weights(i, zcr):
            _boff, _diag, j, r, mk = work[i]
            rs = slice(r * rc, (r + 1) * rc)
            z, c, r_prev = zcr
            a = jnp.exp2(z - c - jnp.concatenate([r_prev] * (kb // LANES), axis=1))
            if mk:
                a = jnp.where(mask_of(i), a, 0.0)
            acc_sc[j, rs, :] += _dot(a.astype(BF16), v_ref[kv_rows(i), :])

        _pipelined(len(work), [scores, suffix, weights], [lag_a, lag_b])
        return r_min[0]

    region(qi * nd, _work_items([(d, d) for d in reversed(range(nd))], nrc, rc, kb, strict=True))
    n_full = qi * nd
    full_work = _work_items([(0, None)], nrc, rc, kb, strict=True)

    def cond(carry):
        i, go = carry
        return jnp.logical_and(i < n_full, go > 0)

    def body(carry):
        i, _ = carry
        r_low = jnp.min(region(n_full - 1 - i, full_work))
        return i + 1, (r_low <= SB_UNDERFLOW_LOG2).astype(jnp.int32)

    lax.while_loop(cond, body, (jnp.int32(0), jnp.int32(1)))
    o_ref[...] = jnp.where(lane < SB_HEAD_DIM, acc_sc[0], acc_sc[1]).astype(o_ref.dtype)


def _sb_attn(q, k, v, tq, kb, rc, lag_a, lag_b):
    b, s, _ = q.shape
    hp = SB_HEADS // 2
    idx = jnp.arange(kb)
    tri = (idx[:, None] >= idx[None, :]).astype(BF16)
    tri2 = jnp.concatenate([tri, tri], axis=0)
    return pl.pallas_call(
        functools.partial(_sb_kernel, tq=tq, kb=kb, rc=rc, lag_a=lag_a, lag_b=lag_b),
        out_shape=jax.ShapeDtypeStruct((b, s, _SBW), BF16),
        grid=(b, hp, s // tq),
        in_specs=[pl.BlockSpec((None, tq, LANES), lambda bi, h, i: (bi, i, h)),
                  pl.BlockSpec((None, s, LANES), lambda bi, h, i: (bi, 0, h)),
                  pl.BlockSpec((None, s, LANES), lambda bi, h, i: (bi, 0, h)),
                  pl.BlockSpec((2 * kb, kb), lambda bi, h, i: (0, 0))],
        out_specs=pl.BlockSpec((None, tq, LANES), lambda bi, h, i: (bi, i, h)),
        scratch_shapes=[pltpu.VMEM((2, tq, LANES), BF16), pltpu.VMEM((2, tq, LANES), F32),
                        pltpu.VMEM((2, tq, LANES), F32)],
        compiler_params=pltpu.CompilerParams(dimension_semantics=("parallel", "parallel", "parallel"),
                                             vmem_limit_bytes=VMEM_LIMIT),
        name="sb_attn",
    )(q, k, v, tri2)


def _mem_kv_kernel(mem_ref, g_ref, w_ref, kv_ref):
    mn = _rms(mem_ref[...], g_ref[...]).astype(BF16)
    kv_ref[...] = _dot(mn, w_ref[...]).astype(BF16)


def _mem_kv(mem2, g_mem, w_xkv):
    rows, _ = mem2.shape
    n = w_xkv.shape[1]
    return pl.pallas_call(
        _mem_kv_kernel,
        out_shape=jax.ShapeDtypeStruct((rows, n), BF16),
        grid=(1,),
        in_specs=[_full(mem2.shape), _full(g_mem.shape), _full(w_xkv.shape)],
        out_specs=_full((rows, n)),
        compiler_params=pltpu.CompilerParams(vmem_limit_bytes=VMEM_LIMIT),
        name="mem_kv",
    )(mem2, g_mem, w_xkv)


def _merge_kernel(x_ref, oa_ref, ob_ref, ga_ref, gb_ref, wa_ref, wb_ref, wo_ref, gx_ref, wxq_ref, kv_ref,
                  wxo_ref, y_ref):
    merged = (ga_ref[...].astype(F32) * _dot(oa_ref[...], wa_ref[...])
              + gb_ref[...].astype(F32) * _dot(ob_ref[...], wb_ref[...]))
    x1 = x_ref[...] + _dot(merged.astype(BF16), wo_ref[...])

    hx = _rms(x1, gx_ref[...]).astype(BF16)
    xw = X_HEADS * X_HEAD_DIM
    xq = (_dot(hx, wxq_ref[...]) * (LOG2E / math.sqrt(X_HEAD_DIM))).astype(BF16)
    heads = []
    for hd in range(X_HEADS):
        sl = slice(hd * X_HEAD_DIM, (hd + 1) * X_HEAD_DIM)
        kh = kv_ref[:, sl]
        vh = kv_ref[:, xw + hd * X_HEAD_DIM: xw + (hd + 1) * X_HEAD_DIM]
        s = _dot_nt(xq[:, sl], kh)
        p = jnp.exp2(s - jnp.max(s, axis=-1, keepdims=True))
        l = jnp.sum(p, axis=-1, keepdims=True)
        heads.append((_dot(p.astype(BF16), vh) / l).astype(BF16))
    xo = jnp.concatenate(heads, axis=1)
    y_ref[...] = x1 + _dot(xo, wxo_ref[...])


def _merge(x2, oa, ob, ga, gb, wa, wb, wo, g_x, wxq, kv, wxo, tm, seq, m_len):
    t, d = x2.shape
    per_b = seq // tm
    row = lambda w: pl.BlockSpec((tm, w), lambda i: (i, 0))
    return pl.pallas_call(
        _merge_kernel,
        out_shape=jax.ShapeDtypeStruct((t, d), F32),
        grid=(t // tm,),
        in_specs=[row(d), row(oa.shape[1]), row(ob.shape[1]), row(d), row(d),
                  _full(wa.shape), _full(wb.shape), _full(wo.shape), _full(g_x.shape), _full(wxq.shape),
                  pl.BlockSpec((m_len, kv.shape[1]), lambda i: (i // per_b, 0)),
                  _full(wxo.shape)],
        out_specs=row(d),
        compiler_params=pltpu.CompilerParams(dimension_semantics=("parallel",), vmem_limit_bytes=VMEM_LIMIT),
        name="merge_xattn",
    )(x2, oa, ob, ga, gb, wa, wb, wo, g_x, wxq, kv, wxo)


def _ffn_kernel(x_ref, g_ref, wg_ref, wu_ref, wd_ref, gf_ref, y_ref, *, n_chunks):
    x = x_ref[...]
    hf = _rms(x, g_ref[...]).astype(BF16)
    d_ff = wg_ref.shape[1]
    cw = d_ff // n_chunks
    y = x
    for c in range(n_chunks):
        sl = slice(c * cw, (c + 1) * cw)
        g = _dot(hf, wg_ref[:, sl])
        u = _dot(hf, wu_ref[:, sl])
        act = (g * jax.nn.sigmoid(g) * u).astype(BF16)
        y = y + _dot(act, wd_ref[sl, :])
    y_ref[...] = _rms(y, gf_ref[...])


def _ffn(x2, g_ffn, wg, wu, wd, g_final, tm, n_chunks):
    t, d = x2.shape
    row = pl.BlockSpec((tm, d), lambda i: (i, 0))
    return pl.pallas_call(
        functools.partial(_ffn_kernel, n_chunks=n_chunks),
        out_shape=jax.ShapeDtypeStruct((t, d), F32),
        grid=(t // tm,),
        in_specs=[row, _full(g_ffn.shape), _full(wg.shape), _full(wu.shape), _full(wd.shape), _full(g_final.shape)],
        out_specs=row,
        compiler_params=pltpu.CompilerParams(dimension_semantics=("parallel",), vmem_limit_bytes=VMEM_LIMIT),
        name="ffn",
    )(x2, g_ffn, wg, wu, wd, g_final)


def kernel(x, mem, positions, g_mix, w_in, b_gate, g_q_lat, w_uq, g_kv_lat, w_ukv, w_a_proj, w_b_proj, w_o,
           g_x, g_mem, w_xq, w_xkv, w_xo, g_ffn, w_gate, w_up, w_down, g_final):
    b, s, d = x.shape
    m_len = mem.shape[1]
    assert g_mix.shape[0] == 1, "the final RMSNorm is fused into the (single) layer's FFN kernel"
    tm = min(512, s)
    tq = min(1024, s)
    kb = min(256, s)
    rc = min(128, s)
    t = b * s

    half = MLA_ROPE // 2
    inv_freq = ROPE_THETA ** (-jnp.arange(0, MLA_ROPE, 2, dtype=F32) / MLA_ROPE)
    invf = jnp.concatenate([jnp.zeros((MLA_NOPE,), F32), inv_freq, inv_freq,
                            jnp.zeros((LANES - MLA_NOPE - 2 * half,), F32)])[None, :]
    pos2 = positions.reshape(t, 1)
    x2 = x.reshape(t, d)
    mem2 = mem.reshape(b * m_len, d)

    w_packed, wuq1, wuq2, wk, wv = _pack_in_weights(w_in[0], w_uq[0], w_ukv[0], d)
    q, k, v, sbq, sbk, sbv, ga, gb = _in_proj(
        x2, pos2, invf, g_mix[0][None, :], w_packed, b_gate[0], g_q_lat[0][None, :], wuq1, wuq2,
        g_kv_lat[0][None, :], wk, wv, tm)
    o_a = _mla_attn(q.reshape(b, s, -1), k.reshape(b, s, -1), v.reshape(b, s, -1), tq, kb, rc, look=3,
                    blocks_per_body=4)
    o_b = _sb_attn(sbq.reshape(b, s, -1), sbk.reshape(b, s, -1), sbv.reshape(b, s, -1), tq, kb, rc,
                   lag_a=3, lag_b=3)
    kv = _mem_kv(mem2, g_mem[0][None, :], w_xkv[0].astype(BF16))
    x2 = _merge(x2, o_a.reshape(t, -1), o_b.reshape(t, -1), ga, gb,
                w_a_proj[0].astype(BF16), w_b_proj[0].astype(BF16), w_o[0].astype(BF16),
                g_x[0][None, :], w_xq[0].astype(BF16), kv, w_xo[0].astype(BF16), tm, s, m_len)
    x2 = _ffn(x2, g_ffn[0][None, :], w_gate[0].astype(BF16), w_up[0].astype(BF16), w_down[0].astype(BF16),
              g_final[None, :], tm, 2)
    return x2.reshape(b, s, d)
```

```python
import functools
import math

import jax
import jax.numpy as jnp
from jax import lax
from jax.experimental import pallas as pl
from jax.experimental.pallas import tpu as pltpu

F32 = jnp.float32
BF16 = jnp.bfloat16

EPS = 1e-6
ROPE_THETA = 10000.0
LOG2E = math.log2(math.e)
LN2 = math.log(2.0)

MLA_HEADS = 8
MLA_Q_RANK = 256
MLA_KV_RANK = 128
MLA_NOPE = 64
MLA_ROPE = 32
MLA_V = 64
SB_HEADS = 8
SB_HEAD_DIM = 64
X_HEADS = 4
X_HEAD_DIM = 128
LANES = 128
SUBLANES = 8
NEG_BIG = -1e30
SB_UNDERFLOW_LOG2 = 160.0
SOFTPLUS_LINEAR_ABOVE = 64.0

VMEM_LIMIT = 56 * 1024 * 1024


def _rms(x, g):
    ms = jnp.mean(x * x, axis=-1, keepdims=True)
    return x * lax.rsqrt(ms + EPS) * g


def _dot(a, b):
    return jnp.dot(a, b, preferred_element_type=F32)


def _dot_nt(a, b):
    return lax.dot_general(a, b, (((1,), (1,)), ((), ())), preferred_element_type=F32)


def _pipelined(n_items, stages, lags):
    offs = [0]
    for lag in lags:
        offs.append(offs[-1] + lag)
    vals = [dict() for _ in stages]
    for tick in range(n_items + offs[-1]):
        for k, stage in enumerate(stages):
            i = tick - offs[k]
            if 0 <= i < n_items:
                prev = vals[k - 1].pop(i) if k > 0 else None
                vals[k][i] = stage(i, prev)


_C_CQ = 0
_C_CKV = _C_CQ + MLA_Q_RANK
_C_KRX = _C_CKV + MLA_KV_RANK
_C_SBQ = _C_KRX + LANES
_SBW = SB_HEADS * SB_HEAD_DIM
_C_SBK = _C_SBQ + _SBW
_C_SBV = _C_SBK + _SBW
_C_GA = _C_SBV + _SBW
_ROPE_END = MLA_NOPE + MLA_ROPE
_POS_GROUPS = LANES // MLA_ROPE


def _in_proj_kernel(x_ref, pos_ref, invf_ref, gmix_ref, win_ref, bg_ref, gq_ref, wuq_ref,
                    gkv_ref, wk_ref, wv_ref,
                    q_ref, k_ref, v_ref, sbq_ref, sbk_ref, sbv_ref, ga_ref, gb_ref, *, d_model):
    x = x_ref[...]
    h = _rms(x, gmix_ref[...]).astype(BF16)

    def proj(c0, width):
        return _dot(h, win_ref[:, c0:c0 + width])

    cq_raw = proj(_C_CQ, MLA_Q_RANK)
    ckv_krx = proj(_C_CKV, MLA_KV_RANK + LANES)
    sbq_ref[...] = (proj(_C_SBQ, _SBW) * (LOG2E / math.sqrt(SB_HEAD_DIM))).astype(BF16)
    sbk_ref[...] = proj(_C_SBK, _SBW).astype(BF16)

    cq = _rms(cq_raw, gq_ref[...]).astype(BF16)
    ckv = _rms(ckv_krx[:, :MLA_KV_RANK], gkv_ref[...]).astype(BF16)
    qt = _dot(cq, wuq_ref[...])
    kn = _dot(ckv, wk_ref[...])
    vv = _dot(ckv, wv_ref[...])
    gate_a = proj(_C_GA, d_model)
    gate_b = proj(_C_GA + d_model, d_model)
    sbv_ref[...] = proj(_C_SBV, _SBW).astype(BF16)

    lane = lax.broadcasted_iota(jnp.int32, (1, LANES), 1)
    ang = pos_ref[...].astype(F32) * invf_ref[...]
    cos_c = jnp.cos(ang)
    sin_c = jnp.sin(ang)
    is_rope = (lane >= MLA_NOPE) & (lane < _ROPE_END)
    cos_parts, sin_parts = [], []
    for g in range(_POS_GROUPS):
        shift = (MLA_NOPE - g * MLA_ROPE) % LANES
        cos_g = cos_c if shift == 0 else pltpu.roll(cos_c, shift, axis=1)
        sin_g = sin_c if shift == 0 else pltpu.roll(sin_c, shift, axis=1)
        cos_parts.append(jnp.where(is_rope, cos_g, jnp.where(lane < MLA_NOPE, 1.0, 0.0)))
        sin_parts.append(jnp.where(is_rope, sin_g, 0.0))
    cos = jnp.concatenate(cos_parts, axis=0)
    sin = jnp.concatenate(sin_parts, axis=0)
    partner_shift = LANES - MLA_ROPE

    def rotary(t, c, s):
        return t * c + pltpu.roll(t, partner_shift, axis=1) * s

    q_scale = LOG2E / math.sqrt(MLA_NOPE + MLA_ROPE)
    cos_q = cos * q_scale
    sin_q = sin * q_scale
    for hd in range(MLA_HEADS):
        sl = slice(hd * LANES, (hd + 1) * LANES)
        q_ref[:, sl] = rotary(qt[:, sl], cos_q, sin_q).astype(BF16)

    krope = rotary(ckv_krx[:, MLA_KV_RANK:], cos, sin)
    for hd in range(MLA_HEADS):
        sl = slice(hd * LANES, (hd + 1) * LANES)
        k_ref[:, sl] = (kn[:, sl] + krope).astype(BF16)

    v_lane = lax.broadcasted_iota(jnp.int32, (1, MLA_HEADS * LANES), 1)
    ones_cols = jnp.where(v_lane % LANES >= MLA_V, 1.0, 0.0).astype(F32)
    v_ref[...] = (vv + ones_cols).astype(BF16)

    ga_ref[...] = jax.nn.sigmoid(gate_a + bg_ref[0:1, :]).astype(BF16)
    gb_ref[...] = jax.nn.sigmoid(gate_b + bg_ref[1:2, :]).astype(BF16)


def _pack_in_weights(w_in, w_uq, w_ukv, d_model):
    f = w_in.dtype
    zeros = lambda n: jnp.zeros((d_model, n), f)
    o_kr = MLA_Q_RANK + MLA_KV_RANK
    o_sb = o_kr + MLA_ROPE
    half = MLA_ROPE // 2
    assert LANES - MLA_NOPE - MLA_ROPE == MLA_ROPE
    kr = w_in[:, o_kr:o_kr + MLA_ROPE]
    kr_rot = jnp.concatenate([-kr[:, half:], kr[:, :half]], axis=1)
    w_packed = jnp.concatenate([
        w_in[:, :o_kr],
        zeros(MLA_NOPE), kr, kr_rot,
        w_in[:, o_sb:],
    ], axis=1).astype(BF16)

    r = w_uq.shape[0]
    wq = w_uq.reshape(r, MLA_HEADS, MLA_NOPE + MLA_ROPE)
    rope = wq[:, :, MLA_NOPE:]
    rope_rot = jnp.concatenate([-rope[:, :, half:], rope[:, :, :half]], axis=2)
    wuq = jnp.concatenate([wq, rope_rot], axis=2).reshape(r, MLA_HEADS * LANES).astype(BF16)

    rk = w_ukv.shape[0]
    wkv = w_ukv.reshape(rk, MLA_HEADS, MLA_NOPE + MLA_V)
    wk = jnp.concatenate([wkv[:, :, :MLA_NOPE], jnp.zeros((rk, MLA_HEADS, LANES - MLA_NOPE), f)], axis=2)
    wk = wk.reshape(rk, MLA_HEADS * LANES).astype(BF16)
    wv = jnp.concatenate([wkv[:, :, MLA_NOPE:], jnp.zeros((rk, MLA_HEADS, LANES - MLA_V), f)], axis=2)
    wv = wv.reshape(rk, MLA_HEADS * LANES).astype(BF16)
    return w_packed, wuq, wk, wv


def _full(shape):
    n = len(shape)
    return pl.BlockSpec(shape, lambda *_: (0,) * n)


def _in_proj(x2, pos_c, invf, g_mix, w_packed, b_gate, g_q, wuq, g_kv, wk, wv, tm):
    t, d = x2.shape
    row = lambda w: pl.BlockSpec((tm, w), lambda i: (i, 0))
    pos_spec = pl.BlockSpec((tm // _POS_GROUPS, LANES), lambda i: (i, 0))
    out_widths = [MLA_HEADS * LANES, MLA_HEADS * LANES, MLA_HEADS * LANES, _SBW, _SBW, _SBW, d, d]
    return pl.pallas_call(
        functools.partial(_in_proj_kernel, d_model=d),
        out_shape=[jax.ShapeDtypeStruct((t, w), BF16) for w in out_widths],
        grid=(t // tm,),
        in_specs=[row(d), pos_spec, _full(invf.shape), _full(g_mix.shape), _full(w_packed.shape),
                  _full(b_gate.shape), _full(g_q.shape), _full(wuq.shape),
                  _full(g_kv.shape), _full(wk.shape), _full(wv.shape)],
        out_specs=[row(w) for w in out_widths],
        compiler_params=pltpu.CompilerParams(dimension_semantics=("parallel",), vmem_limit_bytes=VMEM_LIMIT),
        name="in_proj",
    )(x2, pos_c, invf, g_mix, w_packed, b_gate, g_q, wuq, g_kv, wk, wv)


def _work_items(blocks, n_chunks, rc, kb, strict):
    items = []
    for boff, diag in blocks:
        for j in range(2):
            for r in range(n_chunks):
                if diag is None:
                    items.append((boff, diag, j, r, False))
                    continue
                lo_row, hi_row = r * rc, (r + 1) * rc - 1
                lo_col, hi_col = diag * kb, (diag + 1) * kb - 1
                if (lo_col >= hi_row) if strict else (lo_col > hi_row):
                    continue
                fully_visible = (hi_col < lo_row) if strict else (hi_col <= lo_row)
                items.append((boff, diag, j, r, not fully_visible))
    return items


def _mla_kernel(q_ref, k_ref, v_ref, o_ref, m_sc, acc_sc, *, tq, kb, rc, look, blocks_per_body):
    qi = pl.program_id(2)
    nd = tq // kb
    nrc = tq // rc
    rows = lax.broadcasted_iota(jnp.int32, (rc, kb), 0)
    cols = lax.broadcasted_iota(jnp.int32, (rc, kb), 1)
    m_sc[...] = jnp.full(m_sc.shape, NEG_BIG, F32)
    acc_sc[...] = jnp.zeros(acc_sc.shape, F32)

    def region(base_blk, work):
        def kv_rows(i):
            return pl.ds(pl.multiple_of((base_blk + work[i][0]) * kb, kb), kb)

        def scores(i, _):
            _boff, _diag, j, r, _mk = work[i]
            hs = slice(j * LANES, (j + 1) * LANES)
            return _dot_nt(q_ref[r * rc:(r + 1) * rc, hs], k_ref[kv_rows(i), hs])

        def finish(i, s):
            _boff, diag, j, r, mk = work[i]
            hs = slice(j * LANES, (j + 1) * LANES)
            rs = slice(r * rc, (r + 1) * rc)
            if mk:
                s = jnp.where(cols + (diag * kb - r * rc) <= rows, s, NEG_BIG)
            m_prev = m_sc[j, rs, :]
            m_new = jnp.maximum(m_prev, jnp.max(s, axis=-1, keepdims=True))
            alpha = jnp.exp2(m_prev - m_new)
            p = jnp.exp2(s - jnp.concatenate([m_new] * (kb // LANES), axis=1))
            pv = _dot(p.astype(BF16), v_ref[kv_rows(i), hs])
            acc_sc[j, rs, :] = alpha * acc_sc[j, rs, :] + pv
            m_sc[j, rs, :] = m_new

        _pipelined(len(work), [scores, finish], [look])

    full_work = _work_items([(o, None) for o in range(blocks_per_body)], nrc, rc, kb, strict=False)

    def body(i, carry):
        region(i * blocks_per_body, full_work)
        return carry

    lax.fori_loop(0, qi * (nd // blocks_per_body), body, 0)
    for d in range(nd):
        region(qi * nd, _work_items([(d, d)], nrc, rc, kb, strict=False))

    lane = lax.broadcasted_iota(jnp.int32, (tq, LANES), 1)
    a0 = acc_sc[0]
    a1 = acc_sc[1]
    o0 = a0 / pltpu.roll(a0, MLA_V, axis=1)
    o1 = pltpu.roll(a1, MLA_V, axis=1) / a1
    o_ref[...] = jnp.where(lane < MLA_V, o0, o1).astype(o_ref.dtype)


def _mla_attn(q, k, v, tq, kb, rc, look, blocks_per_body):
    b, s, _ = q.shape
    hp = MLA_HEADS // 2
    assert (tq // kb) % blocks_per_body == 0
    return pl.pallas_call(
        functools.partial(_mla_kernel, tq=tq, kb=kb, rc=rc, look=look, blocks_per_body=blocks_per_body),
        out_shape=jax.ShapeDtypeStruct((b, s, MLA_HEADS * MLA_V), BF16),
        grid=(b, hp, s // tq),
        in_specs=[pl.BlockSpec((None, tq, 2 * LANES), lambda bi, h, i: (bi, i, h)),
                  pl.BlockSpec((None, s, 2 * LANES), lambda bi, h, i: (bi, 0, h)),
                  pl.BlockSpec((None, s, 2 * LANES), lambda bi, h, i: (bi, 0, h))],
        out_specs=pl.BlockSpec((None, tq, LANES), lambda bi, h, i: (bi, i, h)),
        scratch_shapes=[pltpu.VMEM((2, tq, LANES), F32), pltpu.VMEM((2, tq, LANES), F32)],
        compiler_params=pltpu.CompilerParams(dimension_semantics=("parallel", "parallel", "parallel"),
                                             vmem_limit_bytes=VMEM_LIMIT),
        name="mla_attn",
    )(q, k, v)


def _sb_kernel(q_ref, k_ref, v_ref, tri_ref, o_ref, qz_sc, r_sc, acc_sc, *, tq, kb, rc, lag_a, lag_b):
    qi = pl.program_id(2)
    nd = tq // kb
    nrc = tq // rc
    rows = lax.broadcasted_iota(jnp.int32, (rc, kb), 0)
    cols = lax.broadcasted_iota(jnp.int32, (rc, kb), 1)
    lane = lax.broadcasted_iota(jnp.int32, (tq, LANES), 1)
    q_both = q_ref[...]
    for j in range(2):
        in_head = (lane >= j * SB_HEAD_DIM) & (lane < (j + 1) * SB_HEAD_DIM)
        qz_sc[j] = jnp.where(in_head, q_both, jnp.zeros_like(q_both))
    r_sc[...] = jnp.zeros(r_sc.shape, F32)
    acc_sc[...] = jnp.zeros(acc_sc.shape, F32)

    def region(base_blk, work):
        r_min = [None]

        def kv_rows(i):
            return pl.ds(pl.multiple_of((base_blk + work[i][0]) * kb, kb), kb)

        def mask_of(i):
            _boff, diag, _j, r, _mk = work[i]
            return cols + (diag * kb - r * rc) < rows

        def scores(i, _):
            _boff, _diag, j, r, _mk = work[i]
            return _dot_nt(qz_sc[j, r * rc:(r + 1) * rc, :], k_ref[kv_rows(i), :])

        def suffix(i, z):
            _boff, _diag, j, r, mk = work[i]
            rs = slice(r * rc, (r + 1) * rc)
            sp = jnp.where(z > SOFTPLUS_LINEAR_ABOVE, z, jnp.log(1.0 + jnp.exp2(z)) * LOG2E)
            if mk:
                sp = jnp.where(mask_of(i), sp, 0.0)
            c = _dot(sp.astype(BF16), tri_ref[...])
            r_prev = r_sc[j, rs, :]
            r_new = r_prev + jnp.sum(sp, axis=-1, keepdims=True)
            r_sc[j, rs, :] = r_new
            parts = [r_new[t8 * SUBLANES:(t8 + 1) * SUBLANES, :] for t8 in range(rc // SUBLANES)]
            while len(parts) > 1:
                parts = [jnp.minimum(a, b) for a, b in zip(parts[0::2], parts[1::2])]
            r_min[0] = parts[0] if r_min[0] is None else jnp.minimum(r_min[0], parts[0])
            return z, c, r_prev

        def weights(i, zcr):
            _boff, _diag, j, r, mk = work[i]
            rs = slice(r * rc, (r + 1) * rc)
            z, c, r_prev = zcr
            a = jnp.exp2(z - c - jnp.concatenate([r_prev] * (kb // LANES), axis=1))
            if mk:
                a = jnp.where(mask_of(i), a, 0.0)
            acc_sc[j, rs, :] += _dot(a.astype(BF16), v_ref[kv_rows(i), :])

        _pipelined(len(work), [scores, suffix, weights], [lag_a, lag_b])
        return r_min[0]

    region(qi * nd, _work_items([(d, d) for d in reversed(range(nd))], nrc, rc, kb, strict=True))
    n_full = qi * nd
    full_work = _work_items([(0, None)], nrc, rc, kb, strict=True)

    def cond(carry):
        i, go = carry
        return jnp.logical_and(i < n_full, go > 0)

    def body(carry):
        i, _ = carry
        r_low = jnp.min(region(n_full - 1 - i, full_work))
        return i + 1, (r_low <= SB_UNDERFLOW_LOG2).astype(jnp.int32)

    lax.while_loop(cond, body, (jnp.int32(0), jnp.int32(1)))
    o_ref[...] = jnp.where(lane < SB_HEAD_DIM, acc_sc[0], acc_sc[1]).astype(o_ref.dtype)


def _sb_attn(q, k, v, tq, kb, rc, lag_a, lag_b):
    b, s, _ = q.shape
    hp = SB_HEADS // 2
    idx = jnp.arange(kb)
    tri = (idx[:, None] >= idx[None, :]).astype(BF16)
    return pl.pallas_call(
        functools.partial(_sb_kernel, tq=tq, kb=kb, rc=rc, lag_a=lag_a, lag_b=lag_b),
        out_shape=jax.ShapeDtypeStruct((b, s, _SBW), BF16),
        grid=(b, hp, s // tq),
        in_specs=[pl.BlockSpec((None, tq, LANES), lambda bi, h, i: (bi, i, h)),
                  pl.BlockSpec((None, s, LANES), lambda bi, h, i: (bi, 0, h)),
                  pl.BlockSpec((None, s, LANES), lambda bi, h, i: (bi, 0, h)),
                  pl.BlockSpec((kb, kb), lambda bi, h, i: (0, 0))],
        out_specs=pl.BlockSpec((None, tq, LANES), lambda bi, h, i: (bi, i, h)),
        scratch_shapes=[pltpu.VMEM((2, tq, LANES), BF16), pltpu.VMEM((2, tq, LANES), F32),
                        pltpu.VMEM((2, tq, LANES), F32)],
        compiler_params=pltpu.CompilerParams(dimension_semantics=("parallel", "parallel", "parallel"),
                                             vmem_limit_bytes=VMEM_LIMIT),
        name="sb_attn",
    )(q, k, v, tri)


def _mem_kv_kernel(mem_ref, g_ref, w_ref, kv_ref):
    mn = _rms(mem_ref[...], g_ref[...]).astype(BF16)
    kv_ref[...] = _dot(mn, w_ref[...]).astype(BF16)


def _mem_kv(mem2, g_mem, w_xkv):
    rows, _ = mem2.shape
    n = w_xkv.shape[1]
    return pl.pallas_call(
        _mem_kv_kernel,
        out_shape=jax.ShapeDtypeStruct((rows, n), BF16),
        grid=(1,),
        in_specs=[_full(mem2.shape), _full(g_mem.shape), _full(w_xkv.shape)],
        out_specs=_full((rows, n)),
        compiler_params=pltpu.CompilerParams(vmem_limit_bytes=VMEM_LIMIT),
        name="mem_kv",
    )(mem2, g_mem, w_xkv)


def _merge_kernel(x_ref, oa_ref, ob_ref, ga_ref, gb_ref, wa_ref, wb_ref, wo_ref, gx_ref, wxq_ref, kv_ref,
                  wxo_ref, y_ref):
    merged = (ga_ref[...].astype(F32) * _dot(oa_ref[...], wa_ref[...])
              + gb_ref[...].astype(F32) * _dot(ob_ref[...], wb_ref[...]))
    x1 = x_ref[...] + _dot(merged.astype(BF16), wo_ref[...])

    hx = _rms(x1, gx_ref[...]).astype(BF16)
    xw = X_HEADS * X_HEAD_DIM
    xq = (_dot(hx, wxq_ref[...]) * (LOG2E / math.sqrt(X_HEAD_DIM))).astype(BF16)
    heads = []
    for hd in range(X_HEADS):
        sl = slice(hd * X_HEAD_DIM, (hd + 1) * X_HEAD_DIM)
        kh = kv_ref[:, sl]
        vh = kv_ref[:, xw + hd * X_HEAD_DIM: xw + (hd + 1) * X_HEAD_DIM]
        s = _dot_nt(xq[:, sl], kh)
        p = jnp.exp2(s - jnp.max(s, axis=-1, keepdims=True))
        l = jnp.sum(p, axis=-1, keepdims=True)
        heads.append((_dot(p.astype(BF16), vh) / l).astype(BF16))
    xo = jnp.concatenate(heads, axis=1)
    y_ref[...] = x1 + _dot(xo, wxo_ref[...])


def _merge(x2, oa, ob, ga, gb, wa, wb, wo, g_x, wxq, kv, wxo, tm, seq, m_len):
    t, d = x2.shape
    per_b = seq // tm
    row = lambda w: pl.BlockSpec((tm, w), lambda i: (i, 0))
    return pl.pallas_call(
        _merge_kernel,
        out_shape=jax.ShapeDtypeStruct((t, d), F32),
        grid=(t // tm,),
        in_specs=[row(d), row(oa.shape[1]), row(ob.shape[1]), row(d), row(d),
                  _full(wa.shape), _full(wb.shape), _full(wo.shape), _full(g_x.shape), _full(wxq.shape),
                  pl.BlockSpec((m_len, kv.shape[1]), lambda i: (i // per_b, 0)),
                  _full(wxo.shape)],
        out_specs=row(d),
        compiler_params=pltpu.CompilerParams(dimension_semantics=("parallel",), vmem_limit_bytes=VMEM_LIMIT),
        name="merge_xattn",
    )(x2, oa, ob, ga, gb, wa, wb, wo, g_x, wxq, kv, wxo)


def _ffn_kernel(x_ref, g_ref, wg_ref, wu_ref, wd_ref, gf_ref, y_ref, *, n_chunks):
    x = x_ref[...]
    hf = _rms(x, g_ref[...]).astype(BF16)
    d_ff = wg_ref.shape[1]
    cw = d_ff // n_chunks
    y = x
    for c in range(n_chunks):
        sl = slice(c * cw, (c + 1) * cw)
        g = _dot(hf, wg_ref[:, sl])
        u = _dot(hf, wu_ref[:, sl])
        act = (g * jax.nn.sigmoid(g) * u).astype(BF16)
        y = y + _dot(act, wd_ref[sl, :])
    y_ref[...] = _rms(y, gf_ref[...])


def _ffn(x2, g_ffn, wg, wu, wd, g_final, tm, n_chunks):
    t, d = x2.shape
    row = pl.BlockSpec((tm, d), lambda i: (i, 0))
    return pl.pallas_call(
        functools.partial(_ffn_kernel, n_chunks=n_chunks),
        out_shape=jax.ShapeDtypeStruct((t, d), F32),
        grid=(t // tm,),
        in_specs=[row, _full(g_ffn.shape), _full(wg.shape), _full(wu.shape), _full(wd.shape), _full(g_final.shape)],
        out_specs=row,
        compiler_params=pltpu.CompilerParams(dimension_semantics=("parallel",), vmem_limit_bytes=VMEM_LIMIT),
        name="ffn",
    )(x2, g_ffn, wg, wu, wd, g_final)


def kernel(x, mem, positions, g_mix, w_in, b_gate, g_q_lat, w_uq, g_kv_lat, w_ukv, w_a_proj, w_b_proj, w_o,
           g_x, g_mem, w_xq, w_xkv, w_xo, g_ffn, w_gate, w_up, w_down, g_final):
    b, s, d = x.shape
    m_len = mem.shape[1]
    assert g_mix.shape[0] == 1, "the final RMSNorm is fused into the (single) layer's FFN kernel"
    tm = min(512, s)
    tq = min(1024, s)
    kb = min(256, s)
    rc = min(128, s)
    t = b * s

    inv_freq = ROPE_THETA ** (-jnp.arange(0, MLA_ROPE, 2, dtype=F32) / MLA_ROPE)
    invf = jnp.tile(inv_freq, 2 * _POS_GROUPS)[None, :]
    pos_c = positions.reshape(t // tm, _POS_GROUPS, tm // _POS_GROUPS).transpose(0, 2, 1)
    pos_c = jnp.repeat(pos_c.reshape(t // _POS_GROUPS, _POS_GROUPS), MLA_ROPE, axis=1)
    x2 = x.reshape(t, d)
    mem2 = mem.reshape(b * m_len, d)

    w_packed, wuq, wk, wv = _pack_in_weights(w_in[0], w_uq[0], w_ukv[0], d)
    q, k, v, sbq, sbk, sbv, ga, gb = _in_proj(
        x2, pos_c, invf, g_mix[0][None, :], w_packed, b_gate[0], g_q_lat[0][None, :], wuq,
        g_kv_lat[0][None, :], wk, wv, tm)
    o_a = _mla_attn(q.reshape(b, s, -1), k.reshape(b, s, -1), v.reshape(b, s, -1), tq, kb, rc, look=3,
                    blocks_per_body=4)
    o_b = _sb_attn(sbq.reshape(b, s, -1), sbk.reshape(b, s, -1), sbv.reshape(b, s, -1), tq, kb, rc,
                   lag_a=3, lag_b=3)
    kv = _mem_kv(mem2, g_mem[0][None, :], w_xkv[0].astype(BF16))
    x2 = _merge(x2, o_a.reshape(t, -1), o_b.reshape(t, -1), ga, gb,
                w_a_proj[0].astype(BF16), w_b_proj[0].astype(BF16), w_o[0].astype(BF16),
                g_x[0][None, :], w_xq[0].astype(BF16), kv, w_xo[0].astype(BF16), tm, s, m_len)
    x2 = _ffn(x2, g_ffn[0][None, :], w_gate[0].astype(BF16), w_up[0].astype(BF16), w_down[0].astype(BF16),
              g_final[None, :], tm, 1)
    return x2.reshape(b, s, d)
```

```python
import functools
import math

import jax
import jax.numpy as jnp
from jax import lax
from jax.experimental import pallas as pl
from jax.experimental.pallas import tpu as pltpu

F32 = jnp.float32
BF16 = jnp.bfloat16

EPS = 1e-6
ROPE_THETA = 10000.0
LOG2E = math.log2(math.e)
LN2 = math.log(2.0)

MLA_HEADS = 8
MLA_Q_RANK = 256
MLA_KV_RANK = 128
MLA_NOPE = 64
MLA_ROPE = 32
MLA_V = 64
SB_HEADS = 8
SB_HEAD_DIM = 64
X_HEADS = 4
X_HEAD_DIM = 128
LANES = 128
SUBLANES = 8
NEG_BIG = -1e30
SB_UNDERFLOW_LOG2 = 160.0
SOFTPLUS_LINEAR_ABOVE = 64.0

VMEM_LIMIT = 56 * 1024 * 1024


def _rms(x, g):
    ms = jnp.mean(x * x, axis=-1, keepdims=True)
    return x * lax.rsqrt(ms + EPS) * g


def _dot(a, b):
    return jnp.dot(a, b, preferred_element_type=F32)


def _dot_nt(a, b):
    return lax.dot_general(a, b, (((1,), (1,)), ((), ())), preferred_element_type=F32)


def _pipelined(n_items, stages, lags):
    offs = [0]
    for lag in lags:
        offs.append(offs[-1] + lag)
    vals = [dict() for _ in stages]
    for tick in range(n_items + offs[-1]):
        for k, stage in enumerate(stages):
            i = tick - offs[k]
            if 0 <= i < n_items:
                prev = vals[k - 1].pop(i) if k > 0 else None
                vals[k][i] = stage(i, prev)


_C_CQ = 0
_C_CKV = _C_CQ + MLA_Q_RANK
_C_KRX = _C_CKV + MLA_KV_RANK
_C_SBQ = _C_KRX + LANES
_SBW = SB_HEADS * SB_HEAD_DIM
_C_SBK = _C_SBQ + _SBW
_C_SBV = _C_SBK + _SBW
_C_GA = _C_SBV + _SBW
_ROPE_END = MLA_NOPE + MLA_ROPE
_POS_GROUPS = LANES // MLA_ROPE


def _in_proj_kernel(x_ref, pos_ref, invf_ref, gmix_ref, win_ref, bg_ref, gq_ref, wuq_ref,
                    gkv_ref, wk_ref, wv_ref,
                    q_ref, k_ref, vt_ref, sbq_ref, sbk_ref, sbv_ref, ga_ref, gb_ref, *, d_model):
    x = x_ref[...]
    h = _rms(x, gmix_ref[...]).astype(BF16)

    def proj(c0, width):
        return _dot(h, win_ref[:, c0:c0 + width])

    cq_raw = proj(_C_CQ, MLA_Q_RANK)
    ckv_krx = proj(_C_CKV, MLA_KV_RANK + LANES)
    sbq_ref[...] = (proj(_C_SBQ, _SBW) * (LOG2E / math.sqrt(SB_HEAD_DIM))).astype(BF16)
    sbk_ref[...] = proj(_C_SBK, _SBW).astype(BF16)

    cq = _rms(cq_raw, gq_ref[...]).astype(BF16)
    ckv = _rms(ckv_krx[:, :MLA_KV_RANK], gkv_ref[...]).astype(BF16)
    qt = _dot(cq, wuq_ref[...])
    kn = _dot(ckv, wk_ref[...])
    vv = _dot(ckv, wv_ref[...])
    gate_a = proj(_C_GA, d_model)
    gate_b = proj(_C_GA + d_model, d_model)
    sbv_ref[...] = proj(_C_SBV, _SBW).astype(BF16)

    lane = lax.broadcasted_iota(jnp.int32, (1, LANES), 1)
    ang = pos_ref[...].astype(F32) * invf_ref[...]
    cos_c = jnp.cos(ang)
    sin_c = jnp.sin(ang)
    is_rope = (lane >= MLA_NOPE) & (lane < _ROPE_END)
    cos_parts, sin_parts = [], []
    for g in range(_POS_GROUPS):
        shift = (MLA_NOPE - g * MLA_ROPE) % LANES
        cos_g = cos_c if shift == 0 else pltpu.roll(cos_c, shift, axis=1)
        sin_g = sin_c if shift == 0 else pltpu.roll(sin_c, shift, axis=1)
        cos_parts.append(jnp.where(is_rope, cos_g, jnp.where(lane < MLA_NOPE, 1.0, 0.0)))
        sin_parts.append(jnp.where(is_rope, sin_g, 0.0))
    cos = jnp.concatenate(cos_parts, axis=0)
    sin = jnp.concatenate(sin_parts, axis=0)
    partner_shift = LANES - MLA_ROPE

    def rotary(t, c, s):
        return t * c + pltpu.roll(t, partner_shift, axis=1) * s

    q_scale = LOG2E / math.sqrt(MLA_NOPE + MLA_ROPE)
    cos_q = cos * q_scale
    sin_q = sin * q_scale
    for hd in range(MLA_HEADS):
        sl = slice(hd * LANES, (hd + 1) * LANES)
        q_ref[:, sl] = rotary(qt[:, sl], cos_q, sin_q).astype(BF16)

    krope = rotary(ckv_krx[:, MLA_KV_RANK:], cos, sin)
    for hd in range(MLA_HEADS):
        sl = slice(hd * LANES, (hd + 1) * LANES)
        k_ref[:, sl] = (kn[:, sl] + krope).astype(BF16)

    v_lane = lax.broadcasted_iota(jnp.int32, (1, MLA_HEADS * LANES), 1)
    ones_cols = jnp.where(v_lane % LANES >= MLA_V, 1.0, 0.0).astype(F32)
    vv = vv + ones_cols
    n_kblk, _, kb = vt_ref.shape
    for c in range(n_kblk):
        for hd in range(MLA_HEADS):
            sl = slice(hd * LANES, (hd + 1) * LANES)
            vt_ref[c, sl, :] = vv[c * kb:(c + 1) * kb, sl].T.astype(BF16)

    ga_ref[...] = jax.nn.sigmoid(gate_a + bg_ref[0:1, :]).astype(BF16)
    gb_ref[...] = jax.nn.sigmoid(gate_b + bg_ref[1:2, :]).astype(BF16)


def _pack_in_weights(w_in, w_uq, w_ukv, d_model):
    f = w_in.dtype
    zeros = lambda n: jnp.zeros((d_model, n), f)
    o_kr = MLA_Q_RANK + MLA_KV_RANK
    o_sb = o_kr + MLA_ROPE
    half = MLA_ROPE // 2
    assert LANES - MLA_NOPE - MLA_ROPE == MLA_ROPE
    kr = w_in[:, o_kr:o_kr + MLA_ROPE]
    kr_rot = jnp.concatenate([-kr[:, half:], kr[:, :half]], axis=1)
    w_packed = jnp.concatenate([
        w_in[:, :o_kr],
        zeros(MLA_NOPE), kr, kr_rot,
        w_in[:, o_sb:],
    ], axis=1).astype(BF16)

    r = w_uq.shape[0]
    wq = w_uq.reshape(r, MLA_HEADS, MLA_NOPE + MLA_ROPE)
    rope = wq[:, :, MLA_NOPE:]
    rope_rot = jnp.concatenate([-rope[:, :, half:], rope[:, :, :half]], axis=2)
    wuq = jnp.concatenate([wq, rope_rot], axis=2).reshape(r, MLA_HEADS * LANES).astype(BF16)

    rk = w_ukv.shape[0]
    wkv = w_ukv.reshape(rk, MLA_HEADS, MLA_NOPE + MLA_V)
    wk = jnp.concatenate([wkv[:, :, :MLA_NOPE], jnp.zeros((rk, MLA_HEADS, LANES - MLA_NOPE), f)], axis=2)
    wk = wk.reshape(rk, MLA_HEADS * LANES).astype(BF16)
    wv = jnp.concatenate([wkv[:, :, MLA_NOPE:], jnp.zeros((rk, MLA_HEADS, LANES - MLA_V), f)], axis=2)
    wv = wv.reshape(rk, MLA_HEADS * LANES).astype(BF16)
    return w_packed, wuq, wk, wv


def _full(shape):
    n = len(shape)
    return pl.BlockSpec(shape, lambda *_: (0,) * n)


def _in_proj(x2, pos_c, invf, g_mix, w_packed, b_gate, g_q, wuq, g_kv, wk, wv, tm, kb):
    t, d = x2.shape
    row = lambda w: pl.BlockSpec((tm, w), lambda i: (i, 0))
    pos_spec = pl.BlockSpec((tm // _POS_GROUPS, LANES), lambda i: (i, 0))
    hw = MLA_HEADS * LANES
    out_widths = [hw, hw, None, _SBW, _SBW, _SBW, d, d]
    vt_shape = jax.ShapeDtypeStruct((t // kb, hw, kb), BF16)
    vt_spec = pl.BlockSpec((tm // kb, hw, kb), lambda i: (i, 0, 0))
    return pl.pallas_call(
        functools.partial(_in_proj_kernel, d_model=d),
        out_shape=[vt_shape if w is None else jax.ShapeDtypeStruct((t, w), BF16) for w in out_widths],
        grid=(t // tm,),
        in_specs=[row(d), pos_spec, _full(invf.shape), _full(g_mix.shape), _full(w_packed.shape),
                  _full(b_gate.shape), _full(g_q.shape), _full(wuq.shape),
                  _full(g_kv.shape), _full(wk.shape), _full(wv.shape)],
        out_specs=[vt_spec if w is None else row(w) for w in out_widths],
        compiler_params=pltpu.CompilerParams(dimension_semantics=("parallel",), vmem_limit_bytes=VMEM_LIMIT),
        name="in_proj",
    )(x2, pos_c, invf, g_mix, w_packed, b_gate, g_q, wuq, g_kv, wk, wv)


def _work_items(blocks, n_chunks, rc, kb, strict):
    items = []
    for boff, diag in blocks:
        for j in range(2):
            for r in range(n_chunks):
                if diag is None:
                    items.append((boff, diag, j, r, False))
                    continue
                lo_row, hi_row = r * rc, (r + 1) * rc - 1
                lo_col, hi_col = diag * kb, (diag + 1) * kb - 1
                if (lo_col >= hi_row) if strict else (lo_col > hi_row):
                    continue
                fully_visible = (hi_col < lo_row) if strict else (hi_col <= lo_row)
                items.append((boff, diag, j, r, not fully_visible))
    return items


def _tile_rows(x, n):
    return jnp.concatenate([x] * n, axis=0)


def _mla_kernel(q_ref, k_ref, vt_ref, o_ref, m_sc, acc_sc, *, tq, kb, rc, look, blocks_per_body):
    qi = pl.program_id(2)
    nd = tq // kb
    nrc = tq // rc
    key_i = lax.broadcasted_iota(jnp.int32, (kb, rc), 0)
    qry_i = lax.broadcasted_iota(jnp.int32, (kb, rc), 1)
    m_sc[...] = jnp.full(m_sc.shape, NEG_BIG, F32)
    acc_sc[...] = jnp.zeros(acc_sc.shape, F32)

    def region(base_blk, work):
        def scores(i, _):
            boff, _diag, j, r, _mk = work[i]
            hs = slice(j * LANES, (j + 1) * LANES)
            keys = pl.ds(pl.multiple_of((base_blk + boff) * kb, kb), kb)
            return _dot_nt(k_ref[keys, hs], q_ref[r * rc:(r + 1) * rc, hs])

        def finish(i, s):
            boff, diag, j, r, mk = work[i]
            qs = slice(r * rc, (r + 1) * rc)
            if mk:
                s = jnp.where(key_i + (diag * kb - r * rc) <= qry_i, s, NEG_BIG)
            m_prev = m_sc[j, :, qs]
            m_new = jnp.maximum(m_prev, jnp.max(s, axis=0, keepdims=True))
            alpha = jnp.exp2(m_prev - m_new)
            p = jnp.exp2(s - _tile_rows(m_new, kb // SUBLANES))
            pv = _dot(vt_ref[base_blk + boff, j * LANES:(j + 1) * LANES, :], p.astype(BF16))
            acc_sc[j, :, qs] = _tile_rows(alpha, LANES // SUBLANES) * acc_sc[j, :, qs] + pv
            m_sc[j, :, qs] = m_new

        _pipelined(len(work), [scores, finish], [look])

    def full_work(n_blocks):
        return _work_items([(o, None) for o in range(n_blocks)], nrc, rc, kb, strict=False)

    n_full = qi * nd
    n_big = n_full // blocks_per_body

    def body(i, carry):
        region(i * blocks_per_body, full_work(blocks_per_body))
        return carry

    lax.fori_loop(0, n_big, body, 0)
    rem = n_full - n_big * blocks_per_body
    for tiles in range(1, blocks_per_body // nd):
        @pl.when(rem == tiles * nd)
        def _():
            region(n_big * blocks_per_body, full_work(tiles * nd))

    region(n_full, _work_items([(d, d) for d in range(nd)], nrc, rc, kb, strict=False))

    for j in range(2):
        o_ref[j * MLA_V:(j + 1) * MLA_V, :] = (acc_sc[j, :MLA_V, :] / acc_sc[j, MLA_V:, :]).astype(o_ref.dtype)


def _mla_attn(q, k, vt, tq, kb, rc, look, blocks_per_body):
    b, s, _ = q.shape
    hp = MLA_HEADS // 2
    assert blocks_per_body % (tq // kb) == 0
    return pl.pallas_call(
        functools.partial(_mla_kernel, tq=tq, kb=kb, rc=rc, look=look, blocks_per_body=blocks_per_body),
        out_shape=jax.ShapeDtypeStruct((b, MLA_HEADS * MLA_V, s), BF16),
        grid=(b, hp, s // tq),
        in_specs=[pl.BlockSpec((None, tq, 2 * LANES), lambda bi, h, i: (bi, i, h)),
                  pl.BlockSpec((None, s, 2 * LANES), lambda bi, h, i: (bi, 0, h)),
                  pl.BlockSpec((None, s // kb, 2 * LANES, kb), lambda bi, h, i: (bi, 0, h, 0))],
        out_specs=pl.BlockSpec((None, 2 * MLA_V, tq), lambda bi, h, i: (bi, h, i)),
        scratch_shapes=[pltpu.VMEM((2, SUBLANES, tq), F32), pltpu.VMEM((2, LANES, tq), F32)],
        compiler_params=pltpu.CompilerParams(dimension_semantics=("parallel", "parallel", "parallel"),
                                             vmem_limit_bytes=VMEM_LIMIT),
        name="mla_attn",
    )(q, k, vt)


def _sb_kernel(q_ref, k_ref, v_ref, tri_ref, o_ref, qz_sc, r_sc, acc_sc, *, tq, kb, rc, lag_a, lag_b):
    qi = pl.program_id(2)
    nd = tq // kb
    nrc = tq // rc
    rows = lax.broadcasted_iota(jnp.int32, (rc, kb), 0)
    cols = lax.broadcasted_iota(jnp.int32, (rc, kb), 1)
    lane = lax.broadcasted_iota(jnp.int32, (tq, LANES), 1)
    q_both = q_ref[...]
    for j in range(2):
        in_head = (lane >= j * SB_HEAD_DIM) & (lane < (j + 1) * SB_HEAD_DIM)
        qz_sc[j] = jnp.where(in_head, q_both, jnp.zeros_like(q_both))
    r_sc[...] = jnp.zeros(r_sc.shape, F32)
    acc_sc[...] = jnp.zeros(acc_sc.shape, F32)

    def region(base_blk, work):
        r_min = [None]

        def kv_rows(i):
            return pl.ds(pl.multiple_of((base_blk + work[i][0]) * kb, kb), kb)

        def mask_of(i):
            _boff, diag, _j, r, _mk = work[i]
            return cols + (diag * kb - r * rc) < rows

        def scores(i, _):
            _boff, _diag, j, r, _mk = work[i]
            return _dot_nt(qz_sc[j, r * rc:(r + 1) * rc, :], k_ref[kv_rows(i), :])

        def suffix(i, z):
            _boff, _diag, j, r, mk = work[i]
            rs = slice(r * rc, (r + 1) * rc)
            sp = jnp.where(z > SOFTPLUS_LINEAR_ABOVE, z, jnp.log(1.0 + jnp.exp2(z)) * LOG2E)
            if mk:
                sp = jnp.where(mask_of(i), sp, 0.0)
            c = _dot(sp.astype(BF16), tri_ref[...])
            r_prev = r_sc[j, rs, :]
            r_new = r_prev + jnp.sum(sp, axis=-1, keepdims=True)
            r_sc[j, rs, :] = r_new
            parts = [r_new[t8 * SUBLANES:(t8 + 1) * SUBLANES, :] for t8 in range(rc // SUBLANES)]
            while len(parts) > 1:
                parts = [jnp.minimum(a, b) for a, b in zip(parts[0::2], parts[1::2])]
            r_min[0] = parts[0] if r_min[0] is None else jnp.minimum(r_min[0], parts[0])
            return z, c, r_prev

        def weights(i, zcr):
            _boff, _diag, j, r, mk = work[i]
            rs = slice(r * rc, (r + 1) * rc)
            z, c, r_prev = zcr
            a = jnp.exp2(z - c - jnp.concatenate([r_prev] * (kb // LANES), axis=1))
            if mk:
                a = jnp.where(mask_of(i), a, 0.0)
            acc_sc[j, rs, :] += _dot(a.astype(BF16), v_ref[kv_rows(i), :])

        _pipelined(len(work), [scores, suffix, weights], [lag_a, lag_b])
        return r_min[0]

    region(qi * nd, _work_items([(d, d) for d in reversed(range(nd))], nrc, rc, kb, strict=True))
    n_full = qi * nd
    full_work = _work_items([(0, None)], nrc, rc, kb, strict=True)

    def cond(carry):
        i, go = carry
        return jnp.logical_and(i < n_full, go > 0)

    def body(carry):
        i, _ = carry
        r_low = jnp.min(region(n_full - 1 - i, full_work))
        return i + 1, (r_low <= SB_UNDERFLOW_LOG2).astype(jnp.int32)

    lax.while_loop(cond, body, (jnp.int32(0), jnp.int32(1)))
    o_ref[...] = jnp.where(lane < SB_HEAD_DIM, acc_sc[0], acc_sc[1]).astype(o_ref.dtype)


def _sb_attn(q, k, v, tq, kb, rc, lag_a, lag_b):
    b, s, _ = q.shape
    hp = SB_HEADS // 2
    idx = jnp.arange(kb)
    tri = (idx[:, None] >= idx[None, :]).astype(BF16)
    return pl.pallas_call(
        functools.partial(_sb_kernel, tq=tq, kb=kb, rc=rc, lag_a=lag_a, lag_b=lag_b),
        out_shape=jax.ShapeDtypeStruct((b, s, _SBW), BF16),
        grid=(b, hp, s // tq),
        in_specs=[pl.BlockSpec((None, tq, LANES), lambda bi, h, i: (bi, i, h)),
                  pl.BlockSpec((None, s, LANES), lambda bi, h, i: (bi, 0, h)),
                  pl.BlockSpec((None, s, LANES), lambda bi, h, i: (bi, 0, h)),
                  pl.BlockSpec((kb, kb), lambda bi, h, i: (0, 0))],
        out_specs=pl.BlockSpec((None, tq, LANES), lambda bi, h, i: (bi, i, h)),
        scratch_shapes=[pltpu.VMEM((2, tq, LANES), BF16), pltpu.VMEM((2, tq, LANES), F32),
                        pltpu.VMEM((2, tq, LANES), F32)],
        compiler_params=pltpu.CompilerParams(dimension_semantics=("parallel", "parallel", "parallel"),
                                             vmem_limit_bytes=VMEM_LIMIT),
        name="sb_attn",
    )(q, k, v, tri)


def _mem_kv_kernel(mem_ref, g_ref, w_ref, kv_ref):
    mn = _rms(mem_ref[...], g_ref[...]).astype(BF16)
    kv_ref[...] = _dot(mn, w_ref[...]).astype(BF16)


def _mem_kv(mem2, g_mem, w_xkv):
    rows, _ = mem2.shape
    n = w_xkv.shape[1]
    return pl.pallas_call(
        _mem_kv_kernel,
        out_shape=jax.ShapeDtypeStruct((rows, n), BF16),
        grid=(1,),
        in_specs=[_full(mem2.shape), _full(g_mem.shape), _full(w_xkv.shape)],
        out_specs=_full((rows, n)),
        compiler_params=pltpu.CompilerParams(vmem_limit_bytes=VMEM_LIMIT),
        name="mem_kv",
    )(mem2, g_mem, w_xkv)


def _merge_kernel(x_ref, oa_ref, ob_ref, ga_ref, gb_ref, wa_ref, wb_ref, wo_ref, gx_ref, wxq_ref, kv_ref,
                  wxo_ref, y_ref):
    pa = lax.dot_general(oa_ref[...], wa_ref[...], (((0,), (0,)), ((), ())), preferred_element_type=F32)
    merged = ga_ref[...].astype(F32) * pa + gb_ref[...].astype(F32) * _dot(ob_ref[...], wb_ref[...])
    x1 = x_ref[...] + _dot(merged.astype(BF16), wo_ref[...])

    hx = _rms(x1, gx_ref[...]).astype(BF16)
    xw = X_HEADS * X_HEAD_DIM
    xq = (_dot(hx, wxq_ref[...]) * (LOG2E / math.sqrt(X_HEAD_DIM))).astype(BF16)
    heads = []
    for hd in range(X_HEADS):
        sl = slice(hd * X_HEAD_DIM, (hd + 1) * X_HEAD_DIM)
        kh = kv_ref[:, sl]
        vh = kv_ref[:, xw + hd * X_HEAD_DIM: xw + (hd + 1) * X_HEAD_DIM]
        s = _dot_nt(xq[:, sl], kh)
        p = jnp.exp2(s - jnp.max(s, axis=-1, keepdims=True))
        l = jnp.sum(p, axis=-1, keepdims=True)
        heads.append((_dot(p.astype(BF16), vh) / l).astype(BF16))
    xo = jnp.concatenate(heads, axis=1)
    y_ref[...] = x1 + _dot(xo, wxo_ref[...])


def _merge(x2, oa_t, ob, ga, gb, wa, wb, wo, g_x, wxq, kv, wxo, tm, seq, m_len):
    t, d = x2.shape
    per_b = seq // tm
    row = lambda w: pl.BlockSpec((tm, w), lambda i: (i, 0))
    return pl.pallas_call(
        _merge_kernel,
        out_shape=jax.ShapeDtypeStruct((t, d), F32),
        grid=(t // tm,),
        in_specs=[row(d), pl.BlockSpec((None, oa_t.shape[1], tm), lambda i: (i // per_b, 0, i % per_b)),
                  row(ob.shape[1]), row(d), row(d),
                  _full(wa.shape), _full(wb.shape), _full(wo.shape), _full(g_x.shape), _full(wxq.shape),
                  pl.BlockSpec((m_len, kv.shape[1]), lambda i: (i // per_b, 0)),
                  _full(wxo.shape)],
        out_specs=row(d),
        compiler_params=pltpu.CompilerParams(dimension_semantics=("parallel",), vmem_limit_bytes=VMEM_LIMIT),
        name="merge_xattn",
    )(x2, oa_t, ob, ga, gb, wa, wb, wo, g_x, wxq, kv, wxo)


def _ffn_kernel(x_ref, g_ref, wg_ref, wu_ref, wd_ref, gf_ref, y_ref, *, n_chunks):
    x = x_ref[...]
    hf = _rms(x, g_ref[...]).astype(BF16)
    d_ff = wg_ref.shape[1]
    cw = d_ff // n_chunks
    y = x
    for c in range(n_chunks):
        sl = slice(c * cw, (c + 1) * cw)
        g = _dot(hf, wg_ref[:, sl])
        u = _dot(hf, wu_ref[:, sl])
        act = (g * jax.nn.sigmoid(g) * u).astype(BF16)
        y = y + _dot(act, wd_ref[sl, :])
    y_ref[...] = _rms(y, gf_ref[...])


def _ffn(x2, g_ffn, wg, wu, wd, g_final, tm, n_chunks):
    t, d = x2.shape
    row = pl.BlockSpec((tm, d), lambda i: (i, 0))
    return pl.pallas_call(
        functools.partial(_ffn_kernel, n_chunks=n_chunks),
        out_shape=jax.ShapeDtypeStruct((t, d), F32),
        grid=(t // tm,),
        in_specs=[row, _full(g_ffn.shape), _full(wg.shape), _full(wu.shape), _full(wd.shape), _full(g_final.shape)],
        out_specs=row,
        compiler_params=pltpu.CompilerParams(dimension_semantics=("parallel",), vmem_limit_bytes=VMEM_LIMIT),
        name="ffn",
    )(x2, g_ffn, wg, wu, wd, g_final)


def kernel(x, mem, positions, g_mix, w_in, b_gate, g_q_lat, w_uq, g_kv_lat, w_ukv, w_a_proj, w_b_proj, w_o,
           g_x, g_mem, w_xq, w_xkv, w_xo, g_ffn, w_gate, w_up, w_down, g_final):
    b, s, d = x.shape
    m_len = mem.shape[1]
    assert g_mix.shape[0] == 1, "the final RMSNorm is fused into the (single) layer's FFN kernel"
    tm = min(512, s)
    tq = min(1024, s)
    kb = min(256, s)
    rc = min(128, s)
    t = b * s

    inv_freq = ROPE_THETA ** (-jnp.arange(0, MLA_ROPE, 2, dtype=F32) / MLA_ROPE)
    invf = jnp.tile(inv_freq, 2 * _POS_GROUPS)[None, :]
    pos_c = positions.reshape(t // tm, _POS_GROUPS, tm // _POS_GROUPS).transpose(0, 2, 1)
    pos_c = jnp.repeat(pos_c.reshape(t // _POS_GROUPS, _POS_GROUPS), MLA_ROPE, axis=1)
    x2 = x.reshape(t, d)
    mem2 = mem.reshape(b * m_len, d)

    w_packed, wuq, wk, wv = _pack_in_weights(w_in[0], w_uq[0], w_ukv[0], d)
    q, k, vt, sbq, sbk, sbv, ga, gb = _in_proj(
        x2, pos_c, invf, g_mix[0][None, :], w_packed, b_gate[0], g_q_lat[0][None, :], wuq,
        g_kv_lat[0][None, :], wk, wv, tm, kb)
    o_a_t = _mla_attn(q.reshape(b, s, -1), k.reshape(b, s, -1), vt.reshape(b, s // kb, -1, kb), tq, kb, kb,
                      look=6, blocks_per_body=8)
    o_b = _sb_attn(sbq.reshape(b, s, -1), sbk.reshape(b, s, -1), sbv.reshape(b, s, -1), tq, kb, rc,
                   lag_a=3, lag_b=3)
    kv = _mem_kv(mem2, g_mem[0][None, :], w_xkv[0].astype(BF16))
    x2 = _merge(x2, o_a_t, o_b.reshape(t, -1), ga, gb,
                w_a_proj[0].astype(BF16), w_b_proj[0].astype(BF16), w_o[0].astype(BF16),
                g_x[0][None, :], w_xq[0].astype(BF16), kv, w_xo[0].astype(BF16), tm, s, m_len)
    x2 = _ffn(x2, g_ffn[0][None, :], w_gate[0].astype(BF16), w_up[0].astype(BF16), w_down[0].astype(BF16),
              g_final[None, :], tm, 1)
    return x2.reshape(b, s, d)
```

```python
import functools
import math

import jax
import jax.numpy as jnp
from jax import lax
from jax.experimental import pallas as pl
from jax.experimental.pallas import tpu as pltpu

F32 = jnp.float32
BF16 = jnp.bfloat16

EPS = 1e-6
ROPE_THETA = 10000.0
LOG2E = math.log2(math.e)
LN2 = math.log(2.0)

MLA_HEADS = 8
MLA_Q_RANK = 256
MLA_KV_RANK = 128
MLA_NOPE = 64
MLA_ROPE = 32
MLA_V = 64
SB_HEADS = 8
SB_HEAD_DIM = 64
X_HEADS = 4
X_HEAD_DIM = 128
LANES = 128
SUBLANES = 8
NEG_BIG = -1e30
SB_UNDERFLOW_LOG2 = 160.0
SOFTPLUS_LINEAR_ABOVE = 64.0

VMEM_LIMIT = 56 * 1024 * 1024


def _rms(x, g):
    ms = jnp.mean(x * x, axis=-1, keepdims=True)
    return x * lax.rsqrt(ms + EPS) * g


def _dot(a, b):
    return jnp.dot(a, b, preferred_element_type=F32)


def _dot_nt(a, b):
    return lax.dot_general(a, b, (((1,), (1,)), ((), ())), preferred_element_type=F32)


def _pipelined(n_items, stages, lags):
    offs = [0]
    for lag in lags:
        offs.append(offs[-1] + lag)
    vals = [dict() for _ in stages]
    for tick in range(n_items + offs[-1]):
        for k, stage in enumerate(stages):
            i = tick - offs[k]
            if 0 <= i < n_items:
                prev = vals[k - 1].pop(i) if k > 0 else None
                vals[k][i] = stage(i, prev)


_C_CQ = 0
_C_CKV = _C_CQ + MLA_Q_RANK
_C_KRX = _C_CKV + MLA_KV_RANK
_C_SBQ = _C_KRX + LANES
_SBW = SB_HEADS * SB_HEAD_DIM
_C_SBK = _C_SBQ + _SBW
_C_SBV = _C_SBK + _SBW
_C_GA = _C_SBV + _SBW
_ROPE_END = MLA_NOPE + MLA_ROPE
_POS_GROUPS = LANES // MLA_ROPE


def _in_proj_kernel(x_ref, pos_ref, invf_ref, gmix_ref, win_ref, bg_ref, gq_ref, wuq_ref,
                    gkv_ref, wk_ref, wv_ref,
                    q_ref, k_ref, vt_ref, sbq_ref, sbk_ref, sbv_ref, ga_ref, gb_ref, *, d_model):
    x = x_ref[...]
    h = _rms(x, gmix_ref[...]).astype(BF16)

    def proj(c0, width):
        return _dot(h, win_ref[:, c0:c0 + width])

    cq_raw = proj(_C_CQ, MLA_Q_RANK)
    ckv_krx = proj(_C_CKV, MLA_KV_RANK + LANES)
    sbq_ref[...] = (proj(_C_SBQ, _SBW) * (LOG2E / math.sqrt(SB_HEAD_DIM))).astype(BF16)
    sbk_ref[...] = proj(_C_SBK, _SBW).astype(BF16)

    cq = _rms(cq_raw, gq_ref[...]).astype(BF16)
    ckv = _rms(ckv_krx[:, :MLA_KV_RANK], gkv_ref[...]).astype(BF16)
    qt = _dot(cq, wuq_ref[...])
    kn = _dot(ckv, wk_ref[...])
    vv = _dot(ckv, wv_ref[...])
    gate_a = proj(_C_GA, d_model)
    gate_b = proj(_C_GA + d_model, d_model)
    sbv_ref[...] = proj(_C_SBV, _SBW).astype(BF16)

    lane = lax.broadcasted_iota(jnp.int32, (1, LANES), 1)
    ang = pos_ref[...].astype(F32) * invf_ref[...]
    cos_c = jnp.cos(ang)
    sin_c = jnp.sin(ang)
    is_rope = (lane >= MLA_NOPE) & (lane < _ROPE_END)
    cos_parts, sin_parts = [], []
    for g in range(_POS_GROUPS):
        shift = (MLA_NOPE - g * MLA_ROPE) % LANES
        cos_g = cos_c if shift == 0 else pltpu.roll(cos_c, shift, axis=1)
        sin_g = sin_c if shift == 0 else pltpu.roll(sin_c, shift, axis=1)
        cos_parts.append(jnp.where(is_rope, cos_g, jnp.where(lane < MLA_NOPE, 1.0, 0.0)))
        sin_parts.append(jnp.where(is_rope, sin_g, 0.0))
    cos = jnp.concatenate(cos_parts, axis=0)
    sin = jnp.concatenate(sin_parts, axis=0)
    partner_shift = LANES - MLA_ROPE

    def rotary(t, c, s):
        return t * c + pltpu.roll(t, partner_shift, axis=1) * s

    q_scale = LOG2E / math.sqrt(MLA_NOPE + MLA_ROPE)
    cos_q = cos * q_scale
    sin_q = sin * q_scale
    for hd in range(MLA_HEADS):
        sl = slice(hd * LANES, (hd + 1) * LANES)
        q_ref[:, sl] = rotary(qt[:, sl], cos_q, sin_q).astype(BF16)

    krope = rotary(ckv_krx[:, MLA_KV_RANK:], cos, sin)
    for hd in range(MLA_HEADS):
        sl = slice(hd * LANES, (hd + 1) * LANES)
        k_ref[:, sl] = (kn[:, sl] + krope).astype(BF16)

    v_lane = lax.broadcasted_iota(jnp.int32, (1, MLA_HEADS * LANES), 1)
    ones_cols = jnp.where(v_lane % LANES >= MLA_V, 1.0, 0.0).astype(F32)
    vv = vv + ones_cols
    n_kblk, _, kb = vt_ref.shape
    for c in range(n_kblk):
        for hd in range(MLA_HEADS):
            sl = slice(hd * LANES, (hd + 1) * LANES)
            vt_ref[c, sl, :] = vv[c * kb:(c + 1) * kb, sl].T.astype(BF16)

    ga_ref[...] = jax.nn.sigmoid(gate_a + bg_ref[0:1, :]).astype(BF16)
    gb_ref[...] = jax.nn.sigmoid(gate_b + bg_ref[1:2, :]).astype(BF16)


def _pack_in_weights(w_in, w_uq, w_ukv, d_model):
    f = w_in.dtype
    zeros = lambda n: jnp.zeros((d_model, n), f)
    o_kr = MLA_Q_RANK + MLA_KV_RANK
    o_sb = o_kr + MLA_ROPE
    half = MLA_ROPE // 2
    assert LANES - MLA_NOPE - MLA_ROPE == MLA_ROPE
    kr = w_in[:, o_kr:o_kr + MLA_ROPE]
    kr_rot = jnp.concatenate([-kr[:, half:], kr[:, :half]], axis=1)
    w_packed = jnp.concatenate([
        w_in[:, :o_kr],
        zeros(MLA_NOPE), kr, kr_rot,
        w_in[:, o_sb:],
    ], axis=1).astype(BF16)

    r = w_uq.shape[0]
    wq = w_uq.reshape(r, MLA_HEADS, MLA_NOPE + MLA_ROPE)
    rope = wq[:, :, MLA_NOPE:]
    rope_rot = jnp.concatenate([-rope[:, :, half:], rope[:, :, :half]], axis=2)
    wuq = jnp.concatenate([wq, rope_rot], axis=2).reshape(r, MLA_HEADS * LANES).astype(BF16)

    rk = w_ukv.shape[0]
    wkv = w_ukv.reshape(rk, MLA_HEADS, MLA_NOPE + MLA_V)
    wk = jnp.concatenate([wkv[:, :, :MLA_NOPE], jnp.zeros((rk, MLA_HEADS, LANES - MLA_NOPE), f)], axis=2)
    wk = wk.reshape(rk, MLA_HEADS * LANES).astype(BF16)
    wv = jnp.concatenate([wkv[:, :, MLA_NOPE:], jnp.zeros((rk, MLA_HEADS, LANES - MLA_V), f)], axis=2)
    wv = wv.reshape(rk, MLA_HEADS * LANES).astype(BF16)
    return w_packed, wuq, wk, wv


def _full(shape):
    n = len(shape)
    return pl.BlockSpec(shape, lambda *_: (0,) * n)


def _in_proj(x2, pos_c, invf, g_mix, w_packed, b_gate, g_q, wuq, g_kv, wk, wv, tm, kb):
    t, d = x2.shape
    row = lambda w: pl.BlockSpec((tm, w), lambda i: (i, 0))
    pos_spec = pl.BlockSpec((tm // _POS_GROUPS, LANES), lambda i: (i, 0))
    hw = MLA_HEADS * LANES
    out_widths = [hw, hw, None, _SBW, _SBW, _SBW, d, d]
    vt_shape = jax.ShapeDtypeStruct((t // kb, hw, kb), BF16)
    vt_spec = pl.BlockSpec((tm // kb, hw, kb), lambda i: (i, 0, 0))
    return pl.pallas_call(
        functools.partial(_in_proj_kernel, d_model=d),
        out_shape=[vt_shape if w is None else jax.ShapeDtypeStruct((t, w), BF16) for w in out_widths],
        grid=(t // tm,),
        in_specs=[row(d), pos_spec, _full(invf.shape), _full(g_mix.shape), _full(w_packed.shape),
                  _full(b_gate.shape), _full(g_q.shape), _full(wuq.shape),
                  _full(g_kv.shape), _full(wk.shape), _full(wv.shape)],
        out_specs=[vt_spec if w is None else row(w) for w in out_widths],
        compiler_params=pltpu.CompilerParams(dimension_semantics=("parallel",), vmem_limit_bytes=VMEM_LIMIT),
        name="in_proj",
    )(x2, pos_c, invf, g_mix, w_packed, b_gate, g_q, wuq, g_kv, wk, wv)


def _work_items(blocks, n_chunks, rc, kb, strict):
    items = []
    for boff, diag in blocks:
        for j in range(2):
            for r in range(n_chunks):
                if diag is None:
                    items.append((boff, diag, j, r, False))
                    continue
                lo_row, hi_row = r * rc, (r + 1) * rc - 1
                lo_col, hi_col = diag * kb, (diag + 1) * kb - 1
                if (lo_col >= hi_row) if strict else (lo_col > hi_row):
                    continue
                fully_visible = (hi_col < lo_row) if strict else (hi_col <= lo_row)
                items.append((boff, diag, j, r, not fully_visible))
    return items


def _tile_rows(x, n):
    return jnp.concatenate([x] * n, axis=0)


def _mla_kernel(q_ref, k_ref, vt_ref, o_ref, m_sc, acc_sc, *, tq, kb, rc, look, blocks_per_body):
    qi = pl.program_id(2)
    nd = tq // kb
    nrc = tq // rc
    key_i = lax.broadcasted_iota(jnp.int32, (kb, rc), 0)
    qry_i = lax.broadcasted_iota(jnp.int32, (kb, rc), 1)
    m_sc[...] = jnp.full(m_sc.shape, NEG_BIG, F32)
    acc_sc[...] = jnp.zeros(acc_sc.shape, F32)

    def region(base_blk, work):
        def scores(i, _):
            boff, _diag, j, r, _mk = work[i]
            hs = slice(j * LANES, (j + 1) * LANES)
            keys = pl.ds(pl.multiple_of((base_blk + boff) * kb, kb), kb)
            return _dot_nt(k_ref[keys, hs], q_ref[r * rc:(r + 1) * rc, hs])

        def finish(i, s):
            boff, diag, j, r, mk = work[i]
            qs = slice(r * rc, (r + 1) * rc)
            if mk:
                s = jnp.where(key_i + (diag * kb - r * rc) <= qry_i, s, NEG_BIG)
            m_prev = m_sc[j, :, qs]
            m_new = jnp.maximum(m_prev, jnp.max(s, axis=0, keepdims=True))
            alpha = jnp.exp2(m_prev - m_new)
            p = jnp.exp2(s - _tile_rows(m_new, kb // SUBLANES))
            pv = _dot(vt_ref[base_blk + boff, j * LANES:(j + 1) * LANES, :], p.astype(BF16))
            acc_sc[j, :, qs] = _tile_rows(alpha, LANES // SUBLANES) * acc_sc[j, :, qs] + pv
            m_sc[j, :, qs] = m_new

        _pipelined(len(work), [scores, finish], [look])

    def full_work(n_blocks):
        return _work_items([(o, None) for o in range(n_blocks)], nrc, rc, kb, strict=False)

    n_full = qi * nd
    n_big = n_full // blocks_per_body

    def body(i, carry):
        region(i * blocks_per_body, full_work(blocks_per_body))
        return carry

    lax.fori_loop(0, n_big, body, 0)
    rem = n_full - n_big * blocks_per_body
    for tiles in range(1, blocks_per_body // nd):
        @pl.when(rem == tiles * nd)
        def _():
            region(n_big * blocks_per_body, full_work(tiles * nd))

    region(n_full, _work_items([(d, d) for d in range(nd)], nrc, rc, kb, strict=False))

    for j in range(2):
        o_ref[j * MLA_V:(j + 1) * MLA_V, :] = (acc_sc[j, :MLA_V, :] / acc_sc[j, MLA_V:, :]).astype(o_ref.dtype)


def _mla_attn(q, k, vt, tq, kb, rc, look, blocks_per_body):
    b, s, _ = q.shape
    hp = MLA_HEADS // 2
    assert blocks_per_body % (tq // kb) == 0 or (tq // kb) % blocks_per_body == 0
    return pl.pallas_call(
        functools.partial(_mla_kernel, tq=tq, kb=kb, rc=rc, look=look, blocks_per_body=blocks_per_body),
        out_shape=jax.ShapeDtypeStruct((b, MLA_HEADS * MLA_V, s), BF16),
        grid=(b, hp, s // tq),
        in_specs=[pl.BlockSpec((None, tq, 2 * LANES), lambda bi, h, i: (bi, i, h)),
                  pl.BlockSpec((None, s, 2 * LANES), lambda bi, h, i: (bi, 0, h)),
                  pl.BlockSpec((None, s // kb, 2 * LANES, kb), lambda bi, h, i: (bi, 0, h, 0))],
        out_specs=pl.BlockSpec((None, 2 * MLA_V, tq), lambda bi, h, i: (bi, h, i)),
        scratch_shapes=[pltpu.VMEM((2, SUBLANES, tq), F32), pltpu.VMEM((2, LANES, tq), F32)],
        compiler_params=pltpu.CompilerParams(dimension_semantics=("parallel", "parallel", "parallel"),
                                             vmem_limit_bytes=VMEM_LIMIT),
        name="mla_attn",
    )(q, k, vt)


def _sb_kernel(q_ref, k_ref, v_ref, tri_ref, o_ref, qz_sc, r_sc, acc_sc, *, tq, kb, rc, lag_a, lag_b):
    qi = pl.program_id(2)
    nd = tq // kb
    nrc = tq // rc
    rows = lax.broadcasted_iota(jnp.int32, (rc, kb), 0)
    cols = lax.broadcasted_iota(jnp.int32, (rc, kb), 1)
    lane = lax.broadcasted_iota(jnp.int32, (tq, LANES), 1)
    q_both = q_ref[...]
    for j in range(2):
        in_head = (lane >= j * SB_HEAD_DIM) & (lane < (j + 1) * SB_HEAD_DIM)
        qz_sc[j] = jnp.where(in_head, q_both, jnp.zeros_like(q_both))
    r_sc[...] = jnp.zeros(r_sc.shape, F32)
    acc_sc[...] = jnp.zeros(acc_sc.shape, F32)

    def region(base_blk, work, tracked=None):
        r_min = [None]

        def kv_rows(i):
            return pl.ds(pl.multiple_of((base_blk + work[i][0]) * kb, kb), kb)

        def mask_of(i):
            _boff, diag, _j, r, _mk = work[i]
            return cols + (diag * kb - r * rc) < rows

        def scores(i, _):
            _boff, _diag, j, r, _mk = work[i]
            return _dot_nt(qz_sc[j, r * rc:(r + 1) * rc, :], k_ref[kv_rows(i), :])

        def suffix(i, z):
            _boff, _diag, j, r, mk = work[i]
            rs = slice(r * rc, (r + 1) * rc)
            sp = jnp.where(z > SOFTPLUS_LINEAR_ABOVE, z, jnp.log(1.0 + jnp.exp2(z)) * LOG2E)
            if mk:
                sp = jnp.where(mask_of(i), sp, 0.0)
            c = _dot(sp.astype(BF16), tri_ref[...])
            r_prev = r_sc[j, rs, :]
            r_new = r_prev + jnp.sum(sp, axis=-1, keepdims=True)
            r_sc[j, rs, :] = r_new
            if tracked is None or i in tracked:
                parts = [r_new[t8 * SUBLANES:(t8 + 1) * SUBLANES, :] for t8 in range(rc // SUBLANES)]
                while len(parts) > 1:
                    parts = [jnp.minimum(a, b) for a, b in zip(parts[0::2], parts[1::2])]
                r_min[0] = parts[0] if r_min[0] is None else jnp.minimum(r_min[0], parts[0])
            return z, c, r_prev

        def weights(i, zcr):
            _boff, _diag, j, r, mk = work[i]
            rs = slice(r * rc, (r + 1) * rc)
            z, c, r_prev = zcr
            a = jnp.exp2(z - c - jnp.concatenate([r_prev] * (kb // LANES), axis=1))
            if mk:
                a = jnp.where(mask_of(i), a, 0.0)
            acc_sc[j, rs, :] += _dot(a.astype(BF16), v_ref[kv_rows(i), :])

        _pipelined(len(work), [scores, suffix, weights], [lag_a, lag_b])
        return r_min[0]

    diag_items = _work_items([(d, d) for d in reversed(range(nd))], nrc, rc, kb, strict=True)
    own_block = lambda r: ((r + 1) * rc - 1) // kb
    near_items = [it for it in diag_items if it[1] >= own_block(it[3]) - 1]
    far_items = [it for it in diag_items if it[1] < own_block(it[3]) - 1]
    far_chunks = {(it[2], it[3]) for it in far_items}
    deciding = {i for i, it in enumerate(near_items)
                if (it[2], it[3]) in far_chunks and it[1] == own_block(it[3]) - 1}
    near_low = region(qi * nd, near_items, tracked=deciding if far_items else set())
    if far_items:
        @pl.when(jnp.min(near_low) <= SB_UNDERFLOW_LOG2)
        def _():
            region(qi * nd, far_items, tracked=set())

    n_full = qi * nd
    full_work = _work_items([(0, None)], nrc, rc, kb, strict=True)

    def cond(carry):
        i, go = carry
        return jnp.logical_and(i < n_full, go > 0)

    def body(carry):
        i, _ = carry
        r_low = jnp.min(region(n_full - 1 - i, full_work))
        return i + 1, (r_low <= SB_UNDERFLOW_LOG2).astype(jnp.int32)

    lax.while_loop(cond, body, (jnp.int32(0), jnp.int32(1)))
    o_ref[...] = jnp.where(lane < SB_HEAD_DIM, acc_sc[0], acc_sc[1]).astype(o_ref.dtype)


def _sb_attn(q, k, v, tq, kb, rc, lag_a, lag_b):
    b, s, _ = q.shape
    hp = SB_HEADS // 2
    idx = jnp.arange(kb)
    tri = (idx[:, None] >= idx[None, :]).astype(BF16)
    return pl.pallas_call(
        functools.partial(_sb_kernel, tq=tq, kb=kb, rc=rc, lag_a=lag_a, lag_b=lag_b),
        out_shape=jax.ShapeDtypeStruct((b, s, _SBW), BF16),
        grid=(b, hp, s // tq),
        in_specs=[pl.BlockSpec((None, tq, LANES), lambda bi, h, i: (bi, i, h)),
                  pl.BlockSpec((None, s, LANES), lambda bi, h, i: (bi, 0, h)),
                  pl.BlockSpec((None, s, LANES), lambda bi, h, i: (bi, 0, h)),
                  pl.BlockSpec((kb, kb), lambda bi, h, i: (0, 0))],
        out_specs=pl.BlockSpec((None, tq, LANES), lambda bi, h, i: (bi, i, h)),
        scratch_shapes=[pltpu.VMEM((2, tq, LANES), BF16), pltpu.VMEM((2, tq, LANES), F32),
                        pltpu.VMEM((2, tq, LANES), F32)],
        compiler_params=pltpu.CompilerParams(dimension_semantics=("parallel", "parallel", "parallel"),
                                             vmem_limit_bytes=VMEM_LIMIT),
        name="sb_attn",
    )(q, k, v, tri)


def _mem_kv_kernel(mem_ref, g_ref, w_ref, kv_ref):
    mn = _rms(mem_ref[...], g_ref[...]).astype(BF16)
    kv_ref[...] = _dot(mn, w_ref[...]).astype(BF16)


def _mem_kv(mem2, g_mem, w_xkv):
    rows, _ = mem2.shape
    n = w_xkv.shape[1]
    return pl.pallas_call(
        _mem_kv_kernel,
        out_shape=jax.ShapeDtypeStruct((rows, n), BF16),
        grid=(1,),
        in_specs=[_full(mem2.shape), _full(g_mem.shape), _full(w_xkv.shape)],
        out_specs=_full((rows, n)),
        compiler_params=pltpu.CompilerParams(vmem_limit_bytes=VMEM_LIMIT),
        name="mem_kv",
    )(mem2, g_mem, w_xkv)


def _merge_kernel(x_ref, oa_ref, ob_ref, ga_ref, gb_ref, wa_ref, wb_ref, wo_ref, gx_ref, wxq_ref, kv_ref,
                  wxo_ref, y_ref):
    pa = lax.dot_general(oa_ref[...], wa_ref[...], (((0,), (0,)), ((), ())), preferred_element_type=F32)
    merged = ga_ref[...].astype(F32) * pa + gb_ref[...].astype(F32) * _dot(ob_ref[...], wb_ref[...])
    x1 = x_ref[...] + _dot(merged.astype(BF16), wo_ref[...])

    hx = _rms(x1, gx_ref[...]).astype(BF16)
    xw = X_HEADS * X_HEAD_DIM
    xq = (_dot(hx, wxq_ref[...]) * (LOG2E / math.sqrt(X_HEAD_DIM))).astype(BF16)
    heads = []
    for hd in range(X_HEADS):
        sl = slice(hd * X_HEAD_DIM, (hd + 1) * X_HEAD_DIM)
        kh = kv_ref[:, sl]
        vh = kv_ref[:, xw + hd * X_HEAD_DIM: xw + (hd + 1) * X_HEAD_DIM]
        s = _dot_nt(xq[:, sl], kh)
        p = jnp.exp2(s - jnp.max(s, axis=-1, keepdims=True))
        l = jnp.sum(p, axis=-1, keepdims=True)
        heads.append((_dot(p.astype(BF16), vh) / l).astype(BF16))
    xo = jnp.concatenate(heads, axis=1)
    y_ref[...] = x1 + _dot(xo, wxo_ref[...])


def _merge(x2, oa_t, ob, ga, gb, wa, wb, wo, g_x, wxq, kv, wxo, tm, seq, m_len):
    t, d = x2.shape
    per_b = seq // tm
    row = lambda w: pl.BlockSpec((tm, w), lambda i: (i, 0))
    return pl.pallas_call(
        _merge_kernel,
        out_shape=jax.ShapeDtypeStruct((t, d), F32),
        grid=(t // tm,),
        in_specs=[row(d), pl.BlockSpec((None, oa_t.shape[1], tm), lambda i: (i // per_b, 0, i % per_b)),
                  row(ob.shape[1]), row(d), row(d),
                  _full(wa.shape), _full(wb.shape), _full(wo.shape), _full(g_x.shape), _full(wxq.shape),
                  pl.BlockSpec((m_len, kv.shape[1]), lambda i: (i // per_b, 0)),
                  _full(wxo.shape)],
        out_specs=row(d),
        compiler_params=pltpu.CompilerParams(dimension_semantics=("parallel",), vmem_limit_bytes=VMEM_LIMIT),
        name="merge_xattn",
    )(x2, oa_t, ob, ga, gb, wa, wb, wo, g_x, wxq, kv, wxo)


def _ffn_kernel(x_ref, g_ref, wg_ref, wu_ref, wd_ref, gf_ref, y_ref, *, n_chunks):
    x = x_ref[...]
    hf = _rms(x, g_ref[...]).astype(BF16)
    d_ff = wg_ref.shape[1]
    cw = d_ff // n_chunks
    y = x
    for c in range(n_chunks):
        sl = slice(c * cw, (c + 1) * cw)
        g = _dot(hf, wg_ref[:, sl])
        u = _dot(hf, wu_ref[:, sl])
        act = (g * jax.nn.sigmoid(g) * u).astype(BF16)
        y = y + _dot(act, wd_ref[sl, :])
    y_ref[...] = _rms(y, gf_ref[...])


def _ffn(x2, g_ffn, wg, wu, wd, g_final, tm, n_chunks):
    t, d = x2.shape
    row = pl.BlockSpec((tm, d), lambda i: (i, 0))
    return pl.pallas_call(
        functools.partial(_ffn_kernel, n_chunks=n_chunks),
        out_shape=jax.ShapeDtypeStruct((t, d), F32),
        grid=(t // tm,),
        in_specs=[row, _full(g_ffn.shape), _full(wg.shape), _full(wu.shape), _full(wd.shape), _full(g_final.shape)],
        out_specs=row,
        compiler_params=pltpu.CompilerParams(dimension_semantics=("parallel",), vmem_limit_bytes=VMEM_LIMIT),
        name="ffn",
    )(x2, g_ffn, wg, wu, wd, g_final)


def kernel(x, mem, positions, g_mix, w_in, b_gate, g_q_lat, w_uq, g_kv_lat, w_ukv, w_a_proj, w_b_proj, w_o,
           g_x, g_mem, w_xq, w_xkv, w_xo, g_ffn, w_gate, w_up, w_down, g_final):
    b, s, d = x.shape
    m_len = mem.shape[1]
    assert g_mix.shape[0] == 1, "the final RMSNorm is fused into the (single) layer's FFN kernel"
    tm = min(512, s)
    tq = min(1024, s)
    kb = min(256, s)
    rc = min(128, s)
    t = b * s

    inv_freq = ROPE_THETA ** (-jnp.arange(0, MLA_ROPE, 2, dtype=F32) / MLA_ROPE)
    invf = jnp.tile(inv_freq, 2 * _POS_GROUPS)[None, :]
    pos_c = positions.reshape(t // tm, _POS_GROUPS, tm // _POS_GROUPS).transpose(0, 2, 1)
    pos_c = jnp.repeat(pos_c.reshape(t // _POS_GROUPS, _POS_GROUPS), MLA_ROPE, axis=1)
    x2 = x.reshape(t, d)
    mem2 = mem.reshape(b * m_len, d)

    w_packed, wuq, wk, wv = _pack_in_weights(w_in[0], w_uq[0], w_ukv[0], d)
    q, k, vt, sbq, sbk, sbv, ga, gb = _in_proj(
        x2, pos_c, invf, g_mix[0][None, :], w_packed, b_gate[0], g_q_lat[0][None, :], wuq,
        g_kv_lat[0][None, :], wk, wv, tm, kb)
    o_a_t = _mla_attn(q.reshape(b, s, -1), k.reshape(b, s, -1), vt.reshape(b, s // kb, -1, kb), min(2048, s), kb,
                      kb, look=6, blocks_per_body=4)
    o_b = _sb_attn(sbq.reshape(b, s, -1), sbk.reshape(b, s, -1), sbv.reshape(b, s, -1), tq, kb, rc,
                   lag_a=3, lag_b=3)
    kv = _mem_kv(mem2, g_mem[0][None, :], w_xkv[0].astype(BF16))
    x2 = _merge(x2, o_a_t, o_b.reshape(t, -1), ga, gb,
                w_a_proj[0].astype(BF16), w_b_proj[0].astype(BF16), w_o[0].astype(BF16),
                g_x[0][None, :], w_xq[0].astype(BF16), kv, w_xo[0].astype(BF16), tm, s, m_len)
    x2 = _ffn(x2, g_ffn[0][None, :], w_gate[0].astype(BF16), w_up[0].astype(BF16), w_down[0].astype(BF16),
              g_final[None, :], tm, 1)
    return x2.reshape(b, s, d)
```

```python
import functools
import math

import jax
import jax.numpy as jnp
from jax import lax
from jax.experimental import pallas as pl
from jax.experimental.pallas import tpu as pltpu

F32 = jnp.float32
BF16 = jnp.bfloat16

EPS = 1e-6
ROPE_THETA = 10000.0
LOG2E = math.log2(math.e)
LN2 = math.log(2.0)

MLA_HEADS = 8
MLA_Q_RANK = 256
MLA_KV_RANK = 128
MLA_NOPE = 64
MLA_ROPE = 32
MLA_V = 64
SB_HEADS = 8
SB_HEAD_DIM = 64
X_HEADS = 4
X_HEAD_DIM = 128
LANES = 128
SUBLANES = 8
BF16_ROWS = 16
MLA_VT_ROWS = MLA_V + BF16_ROWS
NEG_BIG = -1e30
SB_UNDERFLOW_LOG2 = 160.0
SOFTPLUS_LINEAR_ABOVE = 64.0

VMEM_LIMIT = 56 * 1024 * 1024


def _rms(x, g):
    ms = jnp.mean(x * x, axis=-1, keepdims=True)
    return x * lax.rsqrt(ms + EPS) * g


def _dot(a, b):
    return jnp.dot(a, b, preferred_element_type=F32)


def _dot_nt(a, b):
    return lax.dot_general(a, b, (((1,), (1,)), ((), ())), preferred_element_type=F32)


def _pipelined(n_items, stages, lags):
    offs = [0]
    for lag in lags:
        offs.append(offs[-1] + lag)
    vals = [dict() for _ in stages]
    for tick in range(n_items + offs[-1]):
        for k, stage in enumerate(stages):
            i = tick - offs[k]
            if 0 <= i < n_items:
                prev = vals[k - 1].pop(i) if k > 0 else None
                vals[k][i] = stage(i, prev)


_C_CQ = 0
_C_CKV = _C_CQ + MLA_Q_RANK
_C_KRX = _C_CKV + MLA_KV_RANK
_C_SBQ = _C_KRX + LANES
_SBW = SB_HEADS * SB_HEAD_DIM
_C_SBK = _C_SBQ + _SBW
_C_SBV = _C_SBK + _SBW
_C_GA = _C_SBV + _SBW
_ROPE_END = MLA_NOPE + MLA_ROPE
_POS_GROUPS = LANES // MLA_ROPE


def _in_proj_kernel(x_ref, pos_ref, invf_ref, gmix_ref, win_ref, bg_ref, gq_ref, wuq_ref,
                    gkv_ref, wk_ref, wv_ref,
                    q_ref, k_ref, vt_ref, sbq_ref, sbk_ref, sbv_ref, ga_ref, gb_ref, *, d_model):
    x = x_ref[...]
    h = _rms(x, gmix_ref[...]).astype(BF16)

    def proj(c0, width):
        return _dot(h, win_ref[:, c0:c0 + width])

    cq_raw = proj(_C_CQ, MLA_Q_RANK)
    ckv_krx = proj(_C_CKV, MLA_KV_RANK + LANES)
    sbq_ref[...] = (proj(_C_SBQ, _SBW) * (LOG2E / math.sqrt(SB_HEAD_DIM))).astype(BF16)
    sbk_ref[...] = proj(_C_SBK, _SBW).astype(BF16)

    cq = _rms(cq_raw, gq_ref[...]).astype(BF16)
    ckv = _rms(ckv_krx[:, :MLA_KV_RANK], gkv_ref[...]).astype(BF16)
    qt = _dot(cq, wuq_ref[...])
    kn = _dot(ckv, wk_ref[...])
    vv = _dot(ckv, wv_ref[...])
    gate_a = proj(_C_GA, d_model)
    gate_b = proj(_C_GA + d_model, d_model)
    sbv_ref[...] = proj(_C_SBV, _SBW).astype(BF16)

    lane = lax.broadcasted_iota(jnp.int32, (1, LANES), 1)
    ang = pos_ref[...].astype(F32) * invf_ref[...]
    cos_c = jnp.cos(ang)
    sin_c = jnp.sin(ang)
    is_rope = (lane >= MLA_NOPE) & (lane < _ROPE_END)
    cos_parts, sin_parts = [], []
    for g in range(_POS_GROUPS):
        shift = (MLA_NOPE - g * MLA_ROPE) % LANES
        cos_g = cos_c if shift == 0 else pltpu.roll(cos_c, shift, axis=1)
        sin_g = sin_c if shift == 0 else pltpu.roll(sin_c, shift, axis=1)
        cos_parts.append(jnp.where(is_rope, cos_g, jnp.where(lane < MLA_NOPE, 1.0, 0.0)))
        sin_parts.append(jnp.where(is_rope, sin_g, 0.0))
    cos = jnp.concatenate(cos_parts, axis=0)
    sin = jnp.concatenate(sin_parts, axis=0)
    partner_shift = LANES - MLA_ROPE

    def rotary(t, c, s):
        return t * c + pltpu.roll(t, partner_shift, axis=1) * s

    q_scale = LOG2E / math.sqrt(MLA_NOPE + MLA_ROPE)
    cos_q = cos * q_scale
    sin_q = sin * q_scale
    for hd in range(MLA_HEADS):
        sl = slice(hd * LANES, (hd + 1) * LANES)
        q_ref[:, sl] = rotary(qt[:, sl], cos_q, sin_q).astype(BF16)

    krope = rotary(ckv_krx[:, MLA_KV_RANK:], cos, sin)
    for hd in range(MLA_HEADS):
        sl = slice(hd * LANES, (hd + 1) * LANES)
        k_ref[:, sl] = (kn[:, sl] + krope).astype(BF16)

    v_lane = lax.broadcasted_iota(jnp.int32, (1, MLA_HEADS * LANES), 1)
    ones_cols = jnp.where(v_lane % LANES >= MLA_V, 1.0, 0.0).astype(F32)
    vv = vv + ones_cols
    n_kblk, _, kb = vt_ref.shape
    for c in range(n_kblk):
        for hd in range(MLA_HEADS):
            v_t = vv[c * kb:(c + 1) * kb, hd * LANES:(hd + 1) * LANES].T
            vt_ref[c, hd * MLA_VT_ROWS:(hd + 1) * MLA_VT_ROWS, :] = v_t[:MLA_VT_ROWS].astype(BF16)

    ga_ref[...] = jax.nn.sigmoid(gate_a + bg_ref[0:1, :]).astype(BF16)
    gb_ref[...] = jax.nn.sigmoid(gate_b + bg_ref[1:2, :]).astype(BF16)


def _pack_in_weights(w_in, w_uq, w_ukv, d_model):
    f = w_in.dtype
    zeros = lambda n: jnp.zeros((d_model, n), f)
    o_kr = MLA_Q_RANK + MLA_KV_RANK
    o_sb = o_kr + MLA_ROPE
    half = MLA_ROPE // 2
    assert LANES - MLA_NOPE - MLA_ROPE == MLA_ROPE
    kr = w_in[:, o_kr:o_kr + MLA_ROPE]
    kr_rot = jnp.concatenate([-kr[:, half:], kr[:, :half]], axis=1)
    w_packed = jnp.concatenate([
        w_in[:, :o_kr],
        zeros(MLA_NOPE), kr, kr_rot,
        w_in[:, o_sb:],
    ], axis=1).astype(BF16)

    r = w_uq.shape[0]
    wq = w_uq.reshape(r, MLA_HEADS, MLA_NOPE + MLA_ROPE)
    rope = wq[:, :, MLA_NOPE:]
    rope_rot = jnp.concatenate([-rope[:, :, half:], rope[:, :, :half]], axis=2)
    wuq = jnp.concatenate([wq, rope_rot], axis=2).reshape(r, MLA_HEADS * LANES).astype(BF16)

    rk = w_ukv.shape[0]
    wkv = w_ukv.reshape(rk, MLA_HEADS, MLA_NOPE + MLA_V)
    wk = jnp.concatenate([wkv[:, :, :MLA_NOPE], jnp.zeros((rk, MLA_HEADS, LANES - MLA_NOPE), f)], axis=2)
    wk = wk.reshape(rk, MLA_HEADS * LANES).astype(BF16)
    wv = jnp.concatenate([wkv[:, :, MLA_NOPE:], jnp.zeros((rk, MLA_HEADS, LANES - MLA_V), f)], axis=2)
    wv = wv.reshape(rk, MLA_HEADS * LANES).astype(BF16)
    return w_packed, wuq, wk, wv


def _full(shape):
    n = len(shape)
    return pl.BlockSpec(shape, lambda *_: (0,) * n)


def _in_proj(x2, pos_c, invf, g_mix, w_packed, b_gate, g_q, wuq, g_kv, wk, wv, tm, kb):
    t, d = x2.shape
    row = lambda w: pl.BlockSpec((tm, w), lambda i: (i, 0))
    pos_spec = pl.BlockSpec((tm // _POS_GROUPS, LANES), lambda i: (i, 0))
    hw = MLA_HEADS * LANES
    out_widths = [hw, hw, None, _SBW, _SBW, _SBW, d, d]
    vt_shape = jax.ShapeDtypeStruct((t // kb, MLA_HEADS * MLA_VT_ROWS, kb), BF16)
    vt_spec = pl.BlockSpec((tm // kb, MLA_HEADS * MLA_VT_ROWS, kb), lambda i: (i, 0, 0))
    return pl.pallas_call(
        functools.partial(_in_proj_kernel, d_model=d),
        out_shape=[vt_shape if w is None else jax.ShapeDtypeStruct((t, w), BF16) for w in out_widths],
        grid=(t // tm,),
        in_specs=[row(d), pos_spec, _full(invf.shape), _full(g_mix.shape), _full(w_packed.shape),
                  _full(b_gate.shape), _full(g_q.shape), _full(wuq.shape),
                  _full(g_kv.shape), _full(wk.shape), _full(wv.shape)],
        out_specs=[vt_spec if w is None else row(w) for w in out_widths],
        compiler_params=pltpu.CompilerParams(dimension_semantics=("parallel",), vmem_limit_bytes=VMEM_LIMIT),
        name="in_proj",
    )(x2, pos_c, invf, g_mix, w_packed, b_gate, g_q, wuq, g_kv, wk, wv)


def _work_items(blocks, n_chunks, rc, kb, strict):
    items = []
    for boff, diag in blocks:
        for j in range(2):
            for r in range(n_chunks):
                if diag is None:
                    items.append((boff, diag, j, r, False))
                    continue
                lo_row, hi_row = r * rc, (r + 1) * rc - 1
                lo_col, hi_col = diag * kb, (diag + 1) * kb - 1
                if (lo_col >= hi_row) if strict else (lo_col > hi_row):
                    continue
                fully_visible = (hi_col < lo_row) if strict else (hi_col <= lo_row)
                items.append((boff, diag, j, r, not fully_visible))
    return items


def _tile_rows(x, n):
    return jnp.concatenate([x] * n, axis=0)


def _mla_kernel(q_ref, k_ref, vt_ref, o_ref, m_sc, acc_sc, *, tq, kb, rc, look, blocks_per_body):
    qi = pl.program_id(2)
    nd = tq // kb
    nrc = tq // rc
    key_i = lax.broadcasted_iota(jnp.int32, (kb, rc), 0)
    qry_i = lax.broadcasted_iota(jnp.int32, (kb, rc), 1)
    m_sc[...] = jnp.full(m_sc.shape, NEG_BIG, F32)
    acc_sc[...] = jnp.zeros(acc_sc.shape, F32)

    def region(base_blk, work):
        def scores(i, _):
            boff, _diag, j, r, _mk = work[i]
            hs = slice(j * LANES, (j + 1) * LANES)
            keys = pl.ds(pl.multiple_of((base_blk + boff) * kb, kb), kb)
            return _dot_nt(k_ref[keys, hs], q_ref[r * rc:(r + 1) * rc, hs])

        def finish(i, s):
            boff, diag, j, r, mk = work[i]
            qs = slice(r * rc, (r + 1) * rc)
            if mk:
                s = jnp.where(key_i + (diag * kb - r * rc) <= qry_i, s, NEG_BIG)
            m_prev = m_sc[j, :, qs]
            m_new = jnp.maximum(m_prev, jnp.max(s, axis=0, keepdims=True))
            alpha = jnp.exp2(m_prev - m_new)
            p = jnp.exp2(s - _tile_rows(m_new, kb // SUBLANES))
            v_t = vt_ref[base_blk + boff, j * MLA_VT_ROWS:(j + 1) * MLA_VT_ROWS, :]
            pv = _dot(v_t, p.astype(BF16))
            acc_sc[j, :, qs] = _tile_rows(alpha, MLA_VT_ROWS // SUBLANES) * acc_sc[j, :, qs] + pv
            m_sc[j, :, qs] = m_new

        _pipelined(len(work), [scores, finish], [look])

    def full_work(n_blocks):
        return _work_items([(o, None) for o in range(n_blocks)], nrc, rc, kb, strict=False)

    n_full = qi * nd
    n_big = n_full // blocks_per_body

    def body(i, carry):
        region(i * blocks_per_body, full_work(blocks_per_body))
        return carry

    lax.fori_loop(0, n_big, body, 0)
    rem = n_full - n_big * blocks_per_body
    for tiles in range(1, blocks_per_body // nd):
        @pl.when(rem == tiles * nd)
        def _():
            region(n_big * blocks_per_body, full_work(tiles * nd))

    region(n_full, _work_items([(d, d) for d in range(nd)], nrc, rc, kb, strict=False))

    for j in range(2):
        denom = _tile_rows(acc_sc[j, MLA_V:MLA_V + SUBLANES, :], MLA_V // SUBLANES)
        o_ref[j * MLA_V:(j + 1) * MLA_V, :] = (acc_sc[j, :MLA_V, :] / denom).astype(o_ref.dtype)


def _mla_attn(q, k, vt, tq, kb, rc, look, blocks_per_body):
    b, s, _ = q.shape
    hp = MLA_HEADS // 2
    assert blocks_per_body % (tq // kb) == 0 or (tq // kb) % blocks_per_body == 0
    return pl.pallas_call(
        functools.partial(_mla_kernel, tq=tq, kb=kb, rc=rc, look=look, blocks_per_body=blocks_per_body),
        out_shape=jax.ShapeDtypeStruct((b, MLA_HEADS * MLA_V, s), BF16),
        grid=(b, hp, s // tq),
        in_specs=[pl.BlockSpec((None, tq, 2 * LANES), lambda bi, h, i: (bi, i, h)),
                  pl.BlockSpec((None, s, 2 * LANES), lambda bi, h, i: (bi, 0, h)),
                  pl.BlockSpec((None, s // kb, 2 * MLA_VT_ROWS, kb), lambda bi, h, i: (bi, 0, h, 0))],
        out_specs=pl.BlockSpec((None, 2 * MLA_V, tq), lambda bi, h, i: (bi, h, i)),
        scratch_shapes=[pltpu.VMEM((2, SUBLANES, tq), F32), pltpu.VMEM((2, MLA_VT_ROWS, tq), F32)],
        compiler_params=pltpu.CompilerParams(dimension_semantics=("parallel", "parallel", "parallel"),
                                             vmem_limit_bytes=VMEM_LIMIT),
        name="mla_attn",
    )(q, k, vt)


def _sb_kernel(q_ref, k_ref, v_ref, tri_ref, o_ref, qz_sc, r_sc, acc_sc, *, tq, kb, rc, lag_a, lag_b):
    qi = pl.program_id(2)
    nd = tq // kb
    nrc = tq // rc
    rows = lax.broadcasted_iota(jnp.int32, (rc, kb), 0)
    cols = lax.broadcasted_iota(jnp.int32, (rc, kb), 1)
    lane = lax.broadcasted_iota(jnp.int32, (tq, LANES), 1)
    q_both = q_ref[...]
    for j in range(2):
        in_head = (lane >= j * SB_HEAD_DIM) & (lane < (j + 1) * SB_HEAD_DIM)
        qz_sc[j] = jnp.where(in_head, q_both, jnp.zeros_like(q_both))
    r_sc[...] = jnp.zeros(r_sc.shape, F32)
    acc_sc[...] = jnp.zeros(acc_sc.shape, F32)

    def region(base_blk, work, tracked=None):
        r_min = [None]

        def kv_rows(i):
            return pl.ds(pl.multiple_of((base_blk + work[i][0]) * kb, kb), kb)

        def mask_of(i):
            _boff, diag, _j, r, _mk = work[i]
            return cols + (diag * kb - r * rc) < rows

        def scores(i, _):
            _boff, _diag, j, r, _mk = work[i]
            return _dot_nt(qz_sc[j, r * rc:(r + 1) * rc, :], k_ref[kv_rows(i), :])

        def suffix(i, z):
            _boff, _diag, j, r, mk = work[i]
            rs = slice(r * rc, (r + 1) * rc)
            sp = jnp.where(z > SOFTPLUS_LINEAR_ABOVE, z, jnp.log(1.0 + jnp.exp2(z)) * LOG2E)
            if mk:
                sp = jnp.where(mask_of(i), sp, 0.0)
            c = _dot(sp.astype(BF16), tri_ref[...])
            r_prev = r_sc[j, rs, :]
            r_new = r_prev + jnp.sum(sp, axis=-1, keepdims=True)
            r_sc[j, rs, :] = r_new
            if tracked is None or i in tracked:
                parts = [r_new[t8 * SUBLANES:(t8 + 1) * SUBLANES, :] for t8 in range(rc // SUBLANES)]
                while len(parts) > 1:
                    parts = [jnp.minimum(a, b) for a, b in zip(parts[0::2], parts[1::2])]
                r_min[0] = parts[0] if r_min[0] is None else jnp.minimum(r_min[0], parts[0])
            return z, c, r_prev

        def weights(i, zcr):
            _boff, _diag, j, r, mk = work[i]
            rs = slice(r * rc, (r + 1) * rc)
            z, c, r_prev = zcr
            a = jnp.exp2(z - c - jnp.concatenate([r_prev] * (kb // LANES), axis=1))
            if mk:
                a = jnp.where(mask_of(i), a, 0.0)
            acc_sc[j, rs, :] += _dot(a.astype(BF16), v_ref[kv_rows(i), :])

        _pipelined(len(work), [scores, suffix, weights], [lag_a, lag_b])
        return r_min[0]

    diag_items = _work_items([(d, d) for d in reversed(range(nd))], nrc, rc, kb, strict=True)
    own_block = lambda r: ((r + 1) * rc - 1) // kb
    near_items = [it for it in diag_items if it[1] >= own_block(it[3]) - 1]
    far_items = [it for it in diag_items if it[1] < own_block(it[3]) - 1]
    far_chunks = {(it[2], it[3]) for it in far_items}
    deciding = {i for i, it in enumerate(near_items)
                if (it[2], it[3]) in far_chunks and it[1] == own_block(it[3]) - 1}
    near_low = region(qi * nd, near_items, tracked=deciding if far_items else set())
    if far_items:
        @pl.when(jnp.min(near_low) <= SB_UNDERFLOW_LOG2)
        def _():
            region(qi * nd, far_items, tracked=set())

    n_full = qi * nd
    full_work = _work_items([(0, None)], nrc, rc, kb, strict=True)

    def cond(carry):
        i, go = carry
        return jnp.logical_and(i < n_full, go > 0)

    def body(carry):
        i, _ = carry
        r_low = jnp.min(region(n_full - 1 - i, full_work))
        return i + 1, (r_low <= SB_UNDERFLOW_LOG2).astype(jnp.int32)

    lax.while_loop(cond, body, (jnp.int32(0), jnp.int32(1)))
    o_ref[...] = jnp.where(lane < SB_HEAD_DIM, acc_sc[0], acc_sc[1]).astype(o_ref.dtype)


def _sb_attn(q, k, v, tq, kb, rc, lag_a, lag_b):
    b, s, _ = q.shape
    hp = SB_HEADS // 2
    idx = jnp.arange(kb)
    tri = (idx[:, None] >= idx[None, :]).astype(BF16)
    return pl.pallas_call(
        functools.partial(_sb_kernel, tq=tq, kb=kb, rc=rc, lag_a=lag_a, lag_b=lag_b),
        out_shape=jax.ShapeDtypeStruct((b, s, _SBW), BF16),
        grid=(b, hp, s // tq),
        in_specs=[pl.BlockSpec((None, tq, LANES), lambda bi, h, i: (bi, i, h)),
                  pl.BlockSpec((None, s, LANES), lambda bi, h, i: (bi, 0, h)),
                  pl.BlockSpec((None, s, LANES), lambda bi, h, i: (bi, 0, h)),
                  pl.BlockSpec((kb, kb), lambda bi, h, i: (0, 0))],
        out_specs=pl.BlockSpec((None, tq, LANES), lambda bi, h, i: (bi, i, h)),
        scratch_shapes=[pltpu.VMEM((2, tq, LANES), BF16), pltpu.VMEM((2, tq, LANES), F32),
                        pltpu.VMEM((2, tq, LANES), F32)],
        compiler_params=pltpu.CompilerParams(dimension_semantics=("parallel", "parallel", "parallel"),
                                             vmem_limit_bytes=VMEM_LIMIT),
        name="sb_attn",
    )(q, k, v, tri)


def _mem_kv_kernel(mem_ref, g_ref, w_ref, kv_ref):
    mn = _rms(mem_ref[...], g_ref[...]).astype(BF16)
    kv_ref[...] = _dot(mn, w_ref[...]).astype(BF16)


def _mem_kv(mem2, g_mem, w_xkv):
    rows, _ = mem2.shape
    n = w_xkv.shape[1]
    return pl.pallas_call(
        _mem_kv_kernel,
        out_shape=jax.ShapeDtypeStruct((rows, n), BF16),
        grid=(1,),
        in_specs=[_full(mem2.shape), _full(g_mem.shape), _full(w_xkv.shape)],
        out_specs=_full((rows, n)),
        compiler_params=pltpu.CompilerParams(vmem_limit_bytes=VMEM_LIMIT),
        name="mem_kv",
    )(mem2, g_mem, w_xkv)


def _merge_kernel(x_ref, oa_ref, ob_ref, ga_ref, gb_ref, wa_ref, wb_ref, wo_ref, gx_ref, wxq_ref, kv_ref,
                  wxo_ref, y_ref):
    pa = lax.dot_general(oa_ref[...], wa_ref[...], (((0,), (0,)), ((), ())), preferred_element_type=F32)
    merged = ga_ref[...].astype(F32) * pa + gb_ref[...].astype(F32) * _dot(ob_ref[...], wb_ref[...])
    x1 = x_ref[...] + _dot(merged.astype(BF16), wo_ref[...])

    hx = _rms(x1, gx_ref[...]).astype(BF16)
    xw = X_HEADS * X_HEAD_DIM
    xq = (_dot(hx, wxq_ref[...]) * (LOG2E / math.sqrt(X_HEAD_DIM))).astype(BF16)
    heads = []
    for hd in range(X_HEADS):
        sl = slice(hd * X_HEAD_DIM, (hd + 1) * X_HEAD_DIM)
        kh = kv_ref[:, sl]
        vh = kv_ref[:, xw + hd * X_HEAD_DIM: xw + (hd + 1) * X_HEAD_DIM]
        s = _dot_nt(xq[:, sl], kh)
        p = jnp.exp2(s - jnp.max(s, axis=-1, keepdims=True))
        l = jnp.sum(p, axis=-1, keepdims=True)
        heads.append((_dot(p.astype(BF16), vh) / l).astype(BF16))
    xo = jnp.concatenate(heads, axis=1)
    y_ref[...] = x1 + _dot(xo, wxo_ref[...])


def _merge(x2, oa_t, ob, ga, gb, wa, wb, wo, g_x, wxq, kv, wxo, tm, seq, m_len):
    t, d = x2.shape
    per_b = seq // tm
    row = lambda w: pl.BlockSpec((tm, w), lambda i: (i, 0))
    return pl.pallas_call(
        _merge_kernel,
        out_shape=jax.ShapeDtypeStruct((t, d), F32),
        grid=(t // tm,),
        in_specs=[row(d), pl.BlockSpec((None, oa_t.shape[1], tm), lambda i: (i // per_b, 0, i % per_b)),
                  row(ob.shape[1]), row(d), row(d),
                  _full(wa.shape), _full(wb.shape), _full(wo.shape), _full(g_x.shape), _full(wxq.shape),
                  pl.BlockSpec((m_len, kv.shape[1]), lambda i: (i // per_b, 0)),
                  _full(wxo.shape)],
        out_specs=row(d),
        compiler_params=pltpu.CompilerParams(dimension_semantics=("parallel",), vmem_limit_bytes=VMEM_LIMIT),
        name="merge_xattn",
    )(x2, oa_t, ob, ga, gb, wa, wb, wo, g_x, wxq, kv, wxo)


def _ffn_kernel(x_ref, g_ref, wg_ref, wu_ref, wd_ref, gf_ref, y_ref, *, n_chunks):
    x = x_ref[...]
    hf = _rms(x, g_ref[...]).astype(BF16)
    d_ff = wg_ref.shape[1]
    cw = d_ff // n_chunks
    y = x
    for c in range(n_chunks):
        sl = slice(c * cw, (c + 1) * cw)
        g = _dot(hf, wg_ref[:, sl])
        u = _dot(hf, wu_ref[:, sl])
        act = (g * jax.nn.sigmoid(g) * u).astype(BF16)
        y = y + _dot(act, wd_ref[sl, :])
    y_ref[...] = _rms(y, gf_ref[...])


def _ffn(x2, g_ffn, wg, wu, wd, g_final, tm, n_chunks):
    t, d = x2.shape
    row = pl.BlockSpec((tm, d), lambda i: (i, 0))
    return pl.pallas_call(
        functools.partial(_ffn_kernel, n_chunks=n_chunks),
        out_shape=jax.ShapeDtypeStruct((t, d), F32),
        grid=(t // tm,),
        in_specs=[row, _full(g_ffn.shape), _full(wg.shape), _full(wu.shape), _full(wd.shape), _full(g_final.shape)],
        out_specs=row,
        compiler_params=pltpu.CompilerParams(dimension_semantics=("parallel",), vmem_limit_bytes=VMEM_LIMIT),
        name="ffn",
    )(x2, g_ffn, wg, wu, wd, g_final)


def kernel(x, mem, positions, g_mix, w_in, b_gate, g_q_lat, w_uq, g_kv_lat, w_ukv, w_a_proj, w_b_proj, w_o,
           g_x, g_mem, w_xq, w_xkv, w_xo, g_ffn, w_gate, w_up, w_down, g_final):
    b, s, d = x.shape
    m_len = mem.shape[1]
    assert g_mix.shape[0] == 1, "the final RMSNorm is fused into the (single) layer's FFN kernel"
    tm = min(512, s)
    tq = min(1024, s)
    kb = min(256, s)
    rc = min(128, s)
    t = b * s

    inv_freq = ROPE_THETA ** (-jnp.arange(0, MLA_ROPE, 2, dtype=F32) / MLA_ROPE)
    invf = jnp.tile(inv_freq, 2 * _POS_GROUPS)[None, :]
    pos_c = positions.reshape(t // tm, _POS_GROUPS, tm // _POS_GROUPS).transpose(0, 2, 1)
    pos_c = jnp.repeat(pos_c.reshape(t // _POS_GROUPS, _POS_GROUPS), MLA_ROPE, axis=1)
    x2 = x.reshape(t, d)
    mem2 = mem.reshape(b * m_len, d)

    w_packed, wuq, wk, wv = _pack_in_weights(w_in[0], w_uq[0], w_ukv[0], d)
    q, k, vt, sbq, sbk, sbv, ga, gb = _in_proj(
        x2, pos_c, invf, g_mix[0][None, :], w_packed, b_gate[0], g_q_lat[0][None, :], wuq,
        g_kv_lat[0][None, :], wk, wv, tm, kb)
    o_a_t = _mla_attn(q.reshape(b, s, -1), k.reshape(b, s, -1), vt.reshape(b, s // kb, -1, kb), min(2048, s), kb,
                      kb, look=6, blocks_per_body=4)
    o_b = _sb_attn(sbq.reshape(b, s, -1), sbk.reshape(b, s, -1), sbv.reshape(b, s, -1), tq, kb, rc,
                   lag_a=3, lag_b=3)
    kv = _mem_kv(mem2, g_mem[0][None, :], w_xkv[0].astype(BF16))
    x2 = _merge(x2, o_a_t, o_b.reshape(t, -1), ga, gb,
                w_a_proj[0].astype(BF16), w_b_proj[0].astype(BF16), w_o[0].astype(BF16),
                g_x[0][None, :], w_xq[0].astype(BF16), kv, w_xo[0].astype(BF16), tm, s, m_len)
    x2 = _ffn(x2, g_ffn[0][None, :], w_gate[0].astype(BF16), w_up[0].astype(BF16), w_down[0].astype(BF16),
              g_final[None, :], tm, 1)
    return x2.reshape(b, s, d)
```

```python
import functools
import math

import jax
import jax.numpy as jnp
from jax import lax
from jax.experimental import pallas as pl
from jax.experimental.pallas import tpu as pltpu

F32 = jnp.float32
BF16 = jnp.bfloat16

EPS = 1e-6
ROPE_THETA = 10000.0
LOG2E = math.log2(math.e)
LN2 = math.log(2.0)

MLA_HEADS = 8
MLA_Q_RANK = 256
MLA_KV_RANK = 128
MLA_NOPE = 64
MLA_ROPE = 32
MLA_V = 64
SB_HEADS = 8
SB_HEAD_DIM = 64
X_HEADS = 4
X_HEAD_DIM = 128
LANES = 128
SUBLANES = 8
BF16_ROWS = 16
MLA_VT_ROWS = MLA_V + BF16_ROWS
NEG_BIG = -1e30
SB_UNDERFLOW_LOG2 = 160.0
SOFTPLUS_LINEAR_ABOVE = 64.0

VMEM_LIMIT = 56 * 1024 * 1024


def _rms(x, g):
    ms = jnp.mean(x * x, axis=-1, keepdims=True)
    return x * lax.rsqrt(ms + EPS) * g


def _dot(a, b):
    return jnp.dot(a, b, preferred_element_type=F32)


def _dot_nt(a, b):
    return lax.dot_general(a, b, (((1,), (1,)), ((), ())), preferred_element_type=F32)


def _pipelined(n_items, stages, lags):
    offs = [0]
    for lag in lags:
        offs.append(offs[-1] + lag)
    vals = [dict() for _ in stages]
    for tick in range(n_items + offs[-1]):
        for k, stage in enumerate(stages):
            i = tick - offs[k]
            if 0 <= i < n_items:
                prev = vals[k - 1].pop(i) if k > 0 else None
                vals[k][i] = stage(i, prev)


_C_CQ = 0
_C_CKV = _C_CQ + MLA_Q_RANK
_C_KRX = _C_CKV + MLA_KV_RANK
_C_SBQ = _C_KRX + LANES
_SBW = SB_HEADS * SB_HEAD_DIM
_C_SBK = _C_SBQ + _SBW
_C_SBV = _C_SBK + _SBW
_C_GA = _C_SBV + _SBW
_ROPE_END = MLA_NOPE + MLA_ROPE
_POS_GROUPS = LANES // MLA_ROPE


def _in_proj_kernel(x_ref, pos_ref, invf_ref, gmix_ref, win_ref, bg_ref, gq_ref, wuq_ref,
                    gkv_ref, wk_ref, wv_ref,
                    q_ref, k_ref, vt_ref, sbq_ref, sbk_ref, sbv_ref, ga_ref, gb_ref, *, d_model):
    x = x_ref[...]
    h = _rms(x, gmix_ref[...]).astype(BF16)

    def proj(c0, width):
        return _dot(h, win_ref[:, c0:c0 + width])

    cq_raw = proj(_C_CQ, MLA_Q_RANK)
    ckv_krx = proj(_C_CKV, MLA_KV_RANK + LANES)
    sbq_ref[...] = (proj(_C_SBQ, _SBW) * (LOG2E / math.sqrt(SB_HEAD_DIM))).astype(BF16)
    sbk_ref[...] = proj(_C_SBK, _SBW).astype(BF16)

    cq = _rms(cq_raw, gq_ref[...]).astype(BF16)
    ckv = _rms(ckv_krx[:, :MLA_KV_RANK], gkv_ref[...]).astype(BF16)
    qt = _dot(cq, wuq_ref[...])
    kn = _dot(ckv, wk_ref[...])
    vv = _dot(ckv, wv_ref[...])
    gate_a = proj(_C_GA, d_model)
    gate_b = proj(_C_GA + d_model, d_model)
    sbv_ref[...] = proj(_C_SBV, _SBW).astype(BF16)

    lane = lax.broadcasted_iota(jnp.int32, (1, LANES), 1)
    ang = pos_ref[...].astype(F32) * invf_ref[...]
    cos_c = jnp.cos(ang)
    sin_c = jnp.sin(ang)
    is_rope = (lane >= MLA_NOPE) & (lane < _ROPE_END)
    cos_parts, sin_parts = [], []
    for g in range(_POS_GROUPS):
        shift = (MLA_NOPE - g * MLA_ROPE) % LANES
        cos_g = cos_c if shift == 0 else pltpu.roll(cos_c, shift, axis=1)
        sin_g = sin_c if shift == 0 else pltpu.roll(sin_c, shift, axis=1)
        cos_parts.append(jnp.where(is_rope, cos_g, jnp.where(lane < MLA_NOPE, 1.0, 0.0)))
        sin_parts.append(jnp.where(is_rope, sin_g, 0.0))
    cos = jnp.concatenate(cos_parts, axis=0)
    sin = jnp.concatenate(sin_parts, axis=0)
    partner_shift = LANES - MLA_ROPE

    def rotary(t, c, s):
        return t * c + pltpu.roll(t, partner_shift, axis=1) * s

    q_scale = LOG2E / math.sqrt(MLA_NOPE + MLA_ROPE)
    cos_q = cos * q_scale
    sin_q = sin * q_scale
    for hd in range(MLA_HEADS):
        sl = slice(hd * LANES, (hd + 1) * LANES)
        q_ref[:, sl] = rotary(qt[:, sl], cos_q, sin_q).astype(BF16)

    krope = rotary(ckv_krx[:, MLA_KV_RANK:], cos, sin)
    for hd in range(MLA_HEADS):
        sl = slice(hd * LANES, (hd + 1) * LANES)
        k_ref[:, sl] = (kn[:, sl] + krope).astype(BF16)

    v_lane = lax.broadcasted_iota(jnp.int32, (1, MLA_HEADS * LANES), 1)
    ones_cols = jnp.where(v_lane % LANES >= MLA_V, 1.0, 0.0).astype(F32)
    vv = vv + ones_cols
    n_kblk, _, kb = vt_ref.shape
    for c in range(n_kblk):
        for hd in range(MLA_HEADS):
            v_t = vv[c * kb:(c + 1) * kb, hd * LANES:(hd + 1) * LANES].T
            vt_ref[c, hd * MLA_VT_ROWS:(hd + 1) * MLA_VT_ROWS, :] = v_t[:MLA_VT_ROWS].astype(BF16)

    ga_ref[...] = jax.nn.sigmoid(gate_a + bg_ref[0:1, :]).astype(BF16)
    gb_ref[...] = jax.nn.sigmoid(gate_b + bg_ref[1:2, :]).astype(BF16)


def _pack_in_weights(w_in, w_uq, w_ukv, d_model):
    f = w_in.dtype
    zeros = lambda n: jnp.zeros((d_model, n), f)
    o_kr = MLA_Q_RANK + MLA_KV_RANK
    o_sb = o_kr + MLA_ROPE
    half = MLA_ROPE // 2
    assert LANES - MLA_NOPE - MLA_ROPE == MLA_ROPE
    kr = w_in[:, o_kr:o_kr + MLA_ROPE]
    kr_rot = jnp.concatenate([-kr[:, half:], kr[:, :half]], axis=1)
    w_packed = jnp.concatenate([
        w_in[:, :o_kr],
        zeros(MLA_NOPE), kr, kr_rot,
        w_in[:, o_sb:],
    ], axis=1).astype(BF16)

    r = w_uq.shape[0]
    wq = w_uq.reshape(r, MLA_HEADS, MLA_NOPE + MLA_ROPE)
    rope = wq[:, :, MLA_NOPE:]
    rope_rot = jnp.concatenate([-rope[:, :, half:], rope[:, :, :half]], axis=2)
    wuq = jnp.concatenate([wq, rope_rot], axis=2).reshape(r, MLA_HEADS * LANES).astype(BF16)

    rk = w_ukv.shape[0]
    wkv = w_ukv.reshape(rk, MLA_HEADS, MLA_NOPE + MLA_V)
    wk = jnp.concatenate([wkv[:, :, :MLA_NOPE], jnp.zeros((rk, MLA_HEADS, LANES - MLA_NOPE), f)], axis=2)
    wk = wk.reshape(rk, MLA_HEADS * LANES).astype(BF16)
    wv = jnp.concatenate([wkv[:, :, MLA_NOPE:], jnp.zeros((rk, MLA_HEADS, LANES - MLA_V), f)], axis=2)
    wv = wv.reshape(rk, MLA_HEADS * LANES).astype(BF16)
    return w_packed, wuq, wk, wv


def _full(shape):
    n = len(shape)
    return pl.BlockSpec(shape, lambda *_: (0,) * n, pipeline_mode=pl.Buffered(1))


def _in_proj(x2, pos_c, invf, g_mix, w_packed, b_gate, g_q, wuq, g_kv, wk, wv, tm, kb):
    t, d = x2.shape
    row = lambda w: pl.BlockSpec((tm, w), lambda i: (i, 0))
    pos_spec = pl.BlockSpec((tm // _POS_GROUPS, LANES), lambda i: (i, 0))
    hw = MLA_HEADS * LANES
    out_widths = [hw, hw, None, _SBW, _SBW, _SBW, d, d]
    vt_shape = jax.ShapeDtypeStruct((t // kb, MLA_HEADS * MLA_VT_ROWS, kb), BF16)
    vt_spec = pl.BlockSpec((tm // kb, MLA_HEADS * MLA_VT_ROWS, kb), lambda i: (i, 0, 0))
    return pl.pallas_call(
        functools.partial(_in_proj_kernel, d_model=d),
        out_shape=[vt_shape if w is None else jax.ShapeDtypeStruct((t, w), BF16) for w in out_widths],
        grid=(t // tm,),
        in_specs=[row(d), pos_spec, _full(invf.shape), _full(g_mix.shape), _full(w_packed.shape),
                  _full(b_gate.shape), _full(g_q.shape), _full(wuq.shape),
                  _full(g_kv.shape), _full(wk.shape), _full(wv.shape)],
        out_specs=[vt_spec if w is None else row(w) for w in out_widths],
        compiler_params=pltpu.CompilerParams(dimension_semantics=("parallel",), vmem_limit_bytes=VMEM_LIMIT),
        name="in_proj",
    )(x2, pos_c, invf, g_mix, w_packed, b_gate, g_q, wuq, g_kv, wk, wv)


def _work_items(blocks, n_chunks, rc, kb, strict):
    items = []
    for boff, diag in blocks:
        for j in range(2):
            for r in range(n_chunks):
                if diag is None:
                    items.append((boff, diag, j, r, False))
                    continue
                lo_row, hi_row = r * rc, (r + 1) * rc - 1
                lo_col, hi_col = diag * kb, (diag + 1) * kb - 1
                if (lo_col >= hi_row) if strict else (lo_col > hi_row):
                    continue
                fully_visible = (hi_col < lo_row) if strict else (hi_col <= lo_row)
                items.append((boff, diag, j, r, not fully_visible))
    return items


def _tile_rows(x, n):
    return jnp.concatenate([x] * n, axis=0)


def _mla_kernel(q_ref, k_ref, vt_ref, o_ref, m_sc, acc_sc, *, tq, kb, rc, look, blocks_per_body):
    qi = pl.program_id(2)
    nd = tq // kb
    nrc = tq // rc
    key_i = lax.broadcasted_iota(jnp.int32, (kb, rc), 0)
    qry_i = lax.broadcasted_iota(jnp.int32, (kb, rc), 1)
    m_sc[...] = jnp.full(m_sc.shape, NEG_BIG, F32)
    acc_sc[...] = jnp.zeros(acc_sc.shape, F32)

    def region(base_blk, work):
        def scores(i, _):
            boff, _diag, j, r, _mk = work[i]
            hs = slice(j * LANES, (j + 1) * LANES)
            keys = pl.ds(pl.multiple_of((base_blk + boff) * kb, kb), kb)
            return _dot_nt(k_ref[keys, hs], q_ref[r * rc:(r + 1) * rc, hs])

        def finish(i, s):
            boff, diag, j, r, mk = work[i]
            qs = slice(r * rc, (r + 1) * rc)
            if mk:
                s = jnp.where(key_i + (diag * kb - r * rc) <= qry_i, s, NEG_BIG)
            m_prev = m_sc[j, :, qs]
            m_new = jnp.maximum(m_prev, jnp.max(s, axis=0, keepdims=True))
            alpha = jnp.exp2(m_prev - m_new)
            p = jnp.exp2(s - _tile_rows(m_new, kb // SUBLANES))
            v_t = vt_ref[base_blk + boff, j * MLA_VT_ROWS:(j + 1) * MLA_VT_ROWS, :]
            pv = _dot(v_t, p.astype(BF16))
            acc_sc[j, :, qs] = _tile_rows(alpha, MLA_VT_ROWS // SUBLANES) * acc_sc[j, :, qs] + pv
            m_sc[j, :, qs] = m_new

        _pipelined(len(work), [scores, finish], [look])

    def full_work(n_blocks):
        return _work_items([(o, None) for o in range(n_blocks)], nrc, rc, kb, strict=False)

    n_full = qi * nd
    n_big = n_full // blocks_per_body

    def body(i, carry):
        region(i * blocks_per_body, full_work(blocks_per_body))
        return carry

    lax.fori_loop(0, n_big, body, 0)
    rem = n_full - n_big * blocks_per_body
    for tiles in range(1, blocks_per_body // nd):
        @pl.when(rem == tiles * nd)
        def _():
            region(n_big * blocks_per_body, full_work(tiles * nd))

    region(n_full, _work_items([(d, d) for d in range(nd)], nrc, rc, kb, strict=False))

    for j in range(2):
        denom = _tile_rows(acc_sc[j, MLA_V:MLA_V + SUBLANES, :], MLA_V // SUBLANES)
        o_ref[j * MLA_V:(j + 1) * MLA_V, :] = (acc_sc[j, :MLA_V, :] / denom).astype(o_ref.dtype)


def _mla_attn(q, k, vt, tq, kb, rc, look, blocks_per_body):
    b, s, _ = q.shape
    hp = MLA_HEADS // 2
    assert blocks_per_body % (tq // kb) == 0 or (tq // kb) % blocks_per_body == 0
    return pl.pallas_call(
        functools.partial(_mla_kernel, tq=tq, kb=kb, rc=rc, look=look, blocks_per_body=blocks_per_body),
        out_shape=jax.ShapeDtypeStruct((b, MLA_HEADS * MLA_V, s), BF16),
        grid=(b, hp, s // tq),
        in_specs=[pl.BlockSpec((None, tq, 2 * LANES), lambda bi, h, i: (bi, i, h)),
                  pl.BlockSpec((None, s, 2 * LANES), lambda bi, h, i: (bi, 0, h)),
                  pl.BlockSpec((None, s // kb, 2 * MLA_VT_ROWS, kb), lambda bi, h, i: (bi, 0, h, 0))],
        out_specs=pl.BlockSpec((None, 2 * MLA_V, tq), lambda bi, h, i: (bi, h, i)),
        scratch_shapes=[pltpu.VMEM((2, SUBLANES, tq), F32), pltpu.VMEM((2, MLA_VT_ROWS, tq), F32)],
        compiler_params=pltpu.CompilerParams(dimension_semantics=("parallel", "parallel", "parallel"),
                                             vmem_limit_bytes=VMEM_LIMIT),
        name="mla_attn",
    )(q, k, vt)


def _sb_kernel(q_ref, k_ref, v_ref, tri_ref, o_ref, qz_sc, r_sc, acc_sc, *, tq, kb, rc, lag_a, lag_b):
    qi = pl.program_id(2)
    nd = tq // kb
    nrc = tq // rc
    rows = lax.broadcasted_iota(jnp.int32, (rc, kb), 0)
    cols = lax.broadcasted_iota(jnp.int32, (rc, kb), 1)
    lane = lax.broadcasted_iota(jnp.int32, (tq, LANES), 1)
    q_both = q_ref[...]
    for j in range(2):
        in_head = (lane >= j * SB_HEAD_DIM) & (lane < (j + 1) * SB_HEAD_DIM)
        qz_sc[j] = jnp.where(in_head, q_both, jnp.zeros_like(q_both))
    r_sc[...] = jnp.zeros(r_sc.shape, F32)
    acc_sc[...] = jnp.zeros(acc_sc.shape, F32)

    def region(base_blk, work, tracked=None):
        r_min = [None]

        def kv_rows(i):
            return pl.ds(pl.multiple_of((base_blk + work[i][0]) * kb, kb), kb)

        def mask_of(i):
            _boff, diag, _j, r, _mk = work[i]
            return cols + (diag * kb - r * rc) < rows

        def scores(i, _):
            _boff, _diag, j, r, _mk = work[i]
            return _dot_nt(qz_sc[j, r * rc:(r + 1) * rc, :], k_ref[kv_rows(i), :])

        def suffix(i, z):
            _boff, _diag, j, r, mk = work[i]
            rs = slice(r * rc, (r + 1) * rc)
            sp = jnp.where(z > SOFTPLUS_LINEAR_ABOVE, z, jnp.log(1.0 + jnp.exp2(z)) * LOG2E)
            if mk:
                sp = jnp.where(mask_of(i), sp, 0.0)
            c = _dot(sp.astype(BF16), tri_ref[...])
            r_prev = r_sc[j, rs, :]
            r_new = r_prev + jnp.sum(sp, axis=-1, keepdims=True)
            r_sc[j, rs, :] = r_new
            if tracked is None or i in tracked:
                parts = [r_new[t8 * SUBLANES:(t8 + 1) * SUBLANES, :] for t8 in range(rc // SUBLANES)]
                while len(parts) > 1:
                    parts = [jnp.minimum(a, b) for a, b in zip(parts[0::2], parts[1::2])]
                r_min[0] = parts[0] if r_min[0] is None else jnp.minimum(r_min[0], parts[0])
            return z, c, r_prev

        def weights(i, zcr):
            _boff, _diag, j, r, mk = work[i]
            rs = slice(r * rc, (r + 1) * rc)
            z, c, r_prev = zcr
            a = jnp.exp2(z - c - jnp.concatenate([r_prev] * (kb // LANES), axis=1))
            if mk:
                a = jnp.where(mask_of(i), a, 0.0)
            acc_sc[j, rs, :] += _dot(a.astype(BF16), v_ref[kv_rows(i), :])

        _pipelined(len(work), [scores, suffix, weights], [lag_a, lag_b])
        return r_min[0]

    diag_items = _work_items([(d, d) for d in reversed(range(nd))], nrc, rc, kb, strict=True)
    own_block = lambda r: ((r + 1) * rc - 1) // kb
    near_items = [it for it in diag_items if it[1] >= own_block(it[3]) - 1]
    far_items = [it for it in diag_items if it[1] < own_block(it[3]) - 1]
    far_chunks = {(it[2], it[3]) for it in far_items}
    deciding = {i for i, it in enumerate(near_items)
                if (it[2], it[3]) in far_chunks and it[1] == own_block(it[3]) - 1}
    near_low = region(qi * nd, near_items, tracked=deciding if far_items else set())
    if far_items:
        @pl.when(jnp.min(near_low) <= SB_UNDERFLOW_LOG2)
        def _():
            region(qi * nd, far_items, tracked=set())

    n_full = qi * nd
    full_work = _work_items([(0, None)], nrc, rc, kb, strict=True)

    def cond(carry):
        i, go = carry
        return jnp.logical_and(i < n_full, go > 0)

    def body(carry):
        i, _ = carry
        r_low = jnp.min(region(n_full - 1 - i, full_work))
        return i + 1, (r_low <= SB_UNDERFLOW_LOG2).astype(jnp.int32)

    lax.while_loop(cond, body, (jnp.int32(0), jnp.int32(1)))
    o_ref[...] = jnp.where(lane < SB_HEAD_DIM, acc_sc[0], acc_sc[1]).astype(o_ref.dtype)


def _sb_attn(q, k, v, tq, kb, rc, lag_a, lag_b):
    b, s, _ = q.shape
    hp = SB_HEADS // 2
    idx = jnp.arange(kb)
    tri = (idx[:, None] >= idx[None, :]).astype(BF16)
    return pl.pallas_call(
        functools.partial(_sb_kernel, tq=tq, kb=kb, rc=rc, lag_a=lag_a, lag_b=lag_b),
        out_shape=jax.ShapeDtypeStruct((b, s, _SBW), BF16),
        grid=(b, hp, s // tq),
        in_specs=[pl.BlockSpec((None, tq, LANES), lambda bi, h, i: (bi, i, h)),
                  pl.BlockSpec((None, s, LANES), lambda bi, h, i: (bi, 0, h)),
                  pl.BlockSpec((None, s, LANES), lambda bi, h, i: (bi, 0, h)),
                  pl.BlockSpec((kb, kb), lambda bi, h, i: (0, 0))],
        out_specs=pl.BlockSpec((None, tq, LANES), lambda bi, h, i: (bi, i, h)),
        scratch_shapes=[pltpu.VMEM((2, tq, LANES), BF16), pltpu.VMEM((2, tq, LANES), F32),
                        pltpu.VMEM((2, tq, LANES), F32)],
        compiler_params=pltpu.CompilerParams(dimension_semantics=("parallel", "parallel", "parallel"),
                                             vmem_limit_bytes=VMEM_LIMIT),
        name="sb_attn",
    )(q, k, v, tri)


def _mem_kv_kernel(mem_ref, g_ref, w_ref, kv_ref):
    mn = _rms(mem_ref[...], g_ref[...]).astype(BF16)
    kv_ref[...] = _dot(mn, w_ref[...]).astype(BF16)


def _mem_kv(mem2, g_mem, w_xkv):
    rows, _ = mem2.shape
    n = w_xkv.shape[1]
    return pl.pallas_call(
        _mem_kv_kernel,
        out_shape=jax.ShapeDtypeStruct((rows, n), BF16),
        grid=(1,),
        in_specs=[_full(mem2.shape), _full(g_mem.shape), _full(w_xkv.shape)],
        out_specs=_full((rows, n)),
        compiler_params=pltpu.CompilerParams(vmem_limit_bytes=VMEM_LIMIT),
        name="mem_kv",
    )(mem2, g_mem, w_xkv)


def _merge_kernel(x_ref, oa_ref, ob_ref, ga_ref, gb_ref, wa_ref, wb_ref, wo_ref, gx_ref, wxq_ref, kv_ref,
                  wxo_ref, y_ref):
    pa = lax.dot_general(oa_ref[...], wa_ref[...], (((0,), (0,)), ((), ())), preferred_element_type=F32)
    merged = ga_ref[...].astype(F32) * pa + gb_ref[...].astype(F32) * _dot(ob_ref[...], wb_ref[...])
    x1 = x_ref[...] + _dot(merged.astype(BF16), wo_ref[...])

    hx = _rms(x1, gx_ref[...]).astype(BF16)
    xw = X_HEADS * X_HEAD_DIM
    xq = (_dot(hx, wxq_ref[...]) * (LOG2E / math.sqrt(X_HEAD_DIM))).astype(BF16)
    heads = []
    for hd in range(X_HEADS):
        sl = slice(hd * X_HEAD_DIM, (hd + 1) * X_HEAD_DIM)
        kh = kv_ref[:, sl]
        vh = kv_ref[:, xw + hd * X_HEAD_DIM: xw + (hd + 1) * X_HEAD_DIM]
        s = _dot_nt(xq[:, sl], kh)
        p = jnp.exp2(s - jnp.max(s, axis=-1, keepdims=True))
        l = jnp.sum(p, axis=-1, keepdims=True)
        heads.append((_dot(p.astype(BF16), vh) / l).astype(BF16))
    xo = jnp.concatenate(heads, axis=1)
    y_ref[...] = x1 + _dot(xo, wxo_ref[...])


def _merge(x2, oa_t, ob, ga, gb, wa, wb, wo, g_x, wxq, kv, wxo, tm, seq, m_len):
    t, d = x2.shape
    per_b = seq // tm
    row = lambda w: pl.BlockSpec((tm, w), lambda i: (i, 0))
    return pl.pallas_call(
        _merge_kernel,
        out_shape=jax.ShapeDtypeStruct((t, d), F32),
        grid=(t // tm,),
        in_specs=[row(d), pl.BlockSpec((None, oa_t.shape[1], tm), lambda i: (i // per_b, 0, i % per_b)),
                  row(ob.shape[1]), row(d), row(d),
                  _full(wa.shape), _full(wb.shape), _full(wo.shape), _full(g_x.shape), _full(wxq.shape),
                  pl.BlockSpec((m_len, kv.shape[1]), lambda i: (i // per_b, 0)),
                  _full(wxo.shape)],
        out_specs=row(d),
        compiler_params=pltpu.CompilerParams(dimension_semantics=("parallel",), vmem_limit_bytes=VMEM_LIMIT),
        name="merge_xattn",
    )(x2, oa_t, ob, ga, gb, wa, wb, wo, g_x, wxq, kv, wxo)


def _ffn_kernel(x_ref, g_ref, wg_ref, wu_ref, wd_ref, gf_ref, y_ref, *, n_chunks):
    x = x_ref[...]
    hf = _rms(x, g_ref[...]).astype(BF16)
    d_ff = wg_ref.shape[1]
    cw = d_ff // n_chunks
    y = x
    for c in range(n_chunks):
        sl = slice(c * cw, (c + 1) * cw)
        g = _dot(hf, wg_ref[:, sl])
        u = _dot(hf, wu_ref[:, sl])
        act = (g * jax.nn.sigmoid(g) * u).astype(BF16)
        y = y + _dot(act, wd_ref[sl, :])
    y_ref[...] = _rms(y, gf_ref[...])


def _ffn(x2, g_ffn, wg, wu, wd, g_final, tm, n_chunks):
    t, d = x2.shape
    row = pl.BlockSpec((tm, d), lambda i: (i, 0))
    return pl.pallas_call(
        functools.partial(_ffn_kernel, n_chunks=n_chunks),
        out_shape=jax.ShapeDtypeStruct((t, d), F32),
        grid=(t // tm,),
        in_specs=[row, _full(g_ffn.shape), _full(wg.shape), _full(wu.shape), _full(wd.shape), _full(g_final.shape)],
        out_specs=row,
        compiler_params=pltpu.CompilerParams(dimension_semantics=("parallel",), vmem_limit_bytes=VMEM_LIMIT),
        name="ffn",
    )(x2, g_ffn, wg, wu, wd, g_final)


def kernel(x, mem, positions, g_mix, w_in, b_gate, g_q_lat, w_uq, g_kv_lat, w_ukv, w_a_proj, w_b_proj, w_o,
           g_x, g_mem, w_xq, w_xkv, w_xo, g_ffn, w_gate, w_up, w_down, g_final):
    b, s, d = x.shape
    m_len = mem.shape[1]
    assert g_mix.shape[0] == 1, "the final RMSNorm is fused into the (single) layer's FFN kernel"
    tm = min(1024, s)
    tm_in = min(1024, s)
    tq = min(1024, s)
    kb = min(256, s)
    rc = min(128, s)
    t = b * s

    inv_freq = ROPE_THETA ** (-jnp.arange(0, MLA_ROPE, 2, dtype=F32) / MLA_ROPE)
    invf = jnp.tile(inv_freq, 2 * _POS_GROUPS)[None, :]
    pos_c = positions.reshape(t // tm_in, _POS_GROUPS, tm_in // _POS_GROUPS).transpose(0, 2, 1)
    pos_c = jnp.repeat(pos_c.reshape(t // _POS_GROUPS, _POS_GROUPS), MLA_ROPE, axis=1)
    x2 = x.reshape(t, d)
    mem2 = mem.reshape(b * m_len, d)

    w_packed, wuq, wk, wv = _pack_in_weights(w_in[0], w_uq[0], w_ukv[0], d)
    q, k, vt, sbq, sbk, sbv, ga, gb = _in_proj(
        x2, pos_c, invf, g_mix[0][None, :], w_packed, b_gate[0], g_q_lat[0][None, :], wuq,
        g_kv_lat[0][None, :], wk, wv, tm_in, kb)
    o_a_t = _mla_attn(q.reshape(b, s, -1), k.reshape(b, s, -1), vt.reshape(b, s // kb, -1, kb), min(2048, s), kb,
                      kb, look=6, blocks_per_body=4)
    o_b = _sb_attn(sbq.reshape(b, s, -1), sbk.reshape(b, s, -1), sbv.reshape(b, s, -1), tq, kb, rc,
                   lag_a=3, lag_b=3)
    kv = _mem_kv(mem2, g_mem[0][None, :], w_xkv[0].astype(BF16))
    x2 = _merge(x2, o_a_t, o_b.reshape(t, -1), ga, gb,
                w_a_proj[0].astype(BF16), w_b_proj[0].astype(BF16), w_o[0].astype(BF16),
                g_x[0][None, :], w_xq[0].astype(BF16), kv, w_xo[0].astype(BF16), tm, s, m_len)
    x2 = _ffn(x2, g_ffn[0][None, :], w_gate[0].astype(BF16), w_up[0].astype(BF16), w_down[0].astype(BF16),
              g_final[None, :], tm, 1)
    return x2.reshape(b, s, d)
```

```python
import functools
import math

import jax
import jax.numpy as jnp
from jax import lax
from jax.experimental import pallas as pl
from jax.experimental.pallas import tpu as pltpu

F32 = jnp.float32
BF16 = jnp.bfloat16

EPS = 1e-6
ROPE_THETA = 10000.0
LOG2E = math.log2(math.e)

MLA_HEADS = 8
MLA_Q_RANK = 256
MLA_KV_RANK = 128
MLA_NOPE = 64
MLA_ROPE = 32
MLA_V = 64
SB_HEADS = 8
SB_HEAD_DIM = 64
X_HEADS = 4
X_HEAD_DIM = 128
LANES = 128
SUBLANES = 8
BF16_ROWS = 16
MLA_VT_ROWS = MLA_V + BF16_ROWS
NEG_BIG = -1e30
SB_UNDERFLOW_LOG2 = 160.0
SOFTPLUS_LINEAR_ABOVE = 64.0

VMEM_LIMIT = 56 * 1024 * 1024


def _rms(x, g):
    ms = jnp.mean(x * x, axis=-1, keepdims=True)
    return x * lax.rsqrt(ms + EPS) * g


def _dot(a, b):
    return jnp.dot(a, b, preferred_element_type=F32)


def _dot_nt(a, b):
    return lax.dot_general(a, b, (((1,), (1,)), ((), ())), preferred_element_type=F32)


def _pipelined(n_items, stages, lags):
    offs = [0]
    for lag in lags:
        offs.append(offs[-1] + lag)
    vals = [dict() for _ in stages]
    for tick in range(n_items + offs[-1]):
        for k, stage in enumerate(stages):
            i = tick - offs[k]
            if 0 <= i < n_items:
                prev = vals[k - 1].pop(i) if k > 0 else None
                vals[k][i] = stage(i, prev)


_C_CQ = 0
_C_CKV = _C_CQ + MLA_Q_RANK
_C_KRX = _C_CKV + MLA_KV_RANK
_C_SBQ = _C_KRX + LANES
_SBW = SB_HEADS * SB_HEAD_DIM
_C_SBK = _C_SBQ + _SBW
_C_SBV = _C_SBK + _SBW
_C_GA = _C_SBV + _SBW
_ROPE_END = MLA_NOPE + MLA_ROPE
_POS_GROUPS = LANES // MLA_ROPE


def _in_proj_kernel(x_ref, pos_ref, invf_ref, gmix_ref, win_ref, bg_ref, gq_ref, wuq_ref,
                    gkv_ref, wk_ref, wv_ref,
                    q_ref, k_ref, vt_ref, sbq_ref, sbk_ref, sbv_ref, ga_ref, gb_ref, *, d_model):
    x = x_ref[...]
    h = _rms(x, gmix_ref[...]).astype(BF16)

    def proj(c0, width):
        return _dot(h, win_ref[:, c0:c0 + width])

    cq_raw = proj(_C_CQ, MLA_Q_RANK)
    ckv_krx = proj(_C_CKV, MLA_KV_RANK + LANES)
    sbq_ref[...] = (proj(_C_SBQ, _SBW) * (LOG2E / math.sqrt(SB_HEAD_DIM))).astype(BF16)
    sbk_ref[...] = proj(_C_SBK, _SBW).astype(BF16)

    cq = _rms(cq_raw, gq_ref[...]).astype(BF16)
    ckv = _rms(ckv_krx[:, :MLA_KV_RANK], gkv_ref[...]).astype(BF16)
    qt = _dot(cq, wuq_ref[...])
    kn = _dot(ckv, wk_ref[...])
    vv = _dot(ckv, wv_ref[...])
    gate_a = proj(_C_GA, d_model)
    gate_b = proj(_C_GA + d_model, d_model)
    sbv_ref[...] = proj(_C_SBV, _SBW).astype(BF16)

    lane = lax.broadcasted_iota(jnp.int32, (1, LANES), 1)
    ang = pos_ref[...].astype(F32) * invf_ref[...]
    cos_c = jnp.cos(ang)
    sin_c = jnp.sin(ang)
    is_rope = (lane >= MLA_NOPE) & (lane < _ROPE_END)
    cos_parts, sin_parts = [], []
    for g in range(_POS_GROUPS):
        shift = (MLA_NOPE - g * MLA_ROPE) % LANES
        cos_g = cos_c if shift == 0 else pltpu.roll(cos_c, shift, axis=1)
        sin_g = sin_c if shift == 0 else pltpu.roll(sin_c, shift, axis=1)
        cos_parts.append(jnp.where(is_rope, cos_g, jnp.where(lane < MLA_NOPE, 1.0, 0.0)))
        sin_parts.append(jnp.where(is_rope, sin_g, 0.0))
    cos = jnp.concatenate(cos_parts, axis=0)
    sin = jnp.concatenate(sin_parts, axis=0)
    partner_shift = LANES - MLA_ROPE

    def rotary(t, c, s):
        return t * c + pltpu.roll(t, partner_shift, axis=1) * s

    q_scale = LOG2E / math.sqrt(MLA_NOPE + MLA_ROPE)
    cos_q = cos * q_scale
    sin_q = sin * q_scale
    for hd in range(MLA_HEADS):
        sl = slice(hd * LANES, (hd + 1) * LANES)
        q_ref[:, sl] = rotary(qt[:, sl], cos_q, sin_q).astype(BF16)

    krope = rotary(ckv_krx[:, MLA_KV_RANK:], cos, sin)
    for hd in range(MLA_HEADS):
        sl = slice(hd * LANES, (hd + 1) * LANES)
        k_ref[:, sl] = (kn[:, sl] + krope).astype(BF16)

    v_lane = lax.broadcasted_iota(jnp.int32, (1, MLA_HEADS * LANES), 1)
    ones_cols = jnp.where(v_lane % LANES >= MLA_V, 1.0, 0.0).astype(F32)
    vv = vv + ones_cols
    n_kblk, _, kb = vt_ref.shape
    for c in range(n_kblk):
        for hd in range(MLA_HEADS):
            v_t = vv[c * kb:(c + 1) * kb, hd * LANES:(hd + 1) * LANES].T
            vt_ref[c, hd * MLA_VT_ROWS:(hd + 1) * MLA_VT_ROWS, :] = v_t[:MLA_VT_ROWS].astype(BF16)

    ga_ref[...] = jax.nn.sigmoid(gate_a + bg_ref[0:1, :]).astype(BF16)
    gb_ref[...] = jax.nn.sigmoid(gate_b + bg_ref[1:2, :]).astype(BF16)


def _pack_in_weights(w_in, w_uq, w_ukv, d_model):
    f = w_in.dtype
    zeros = lambda n: jnp.zeros((d_model, n), f)
    o_kr = MLA_Q_RANK + MLA_KV_RANK
    o_sb = o_kr + MLA_ROPE
    half = MLA_ROPE // 2
    assert LANES - MLA_NOPE - MLA_ROPE == MLA_ROPE
    kr = w_in[:, o_kr:o_kr + MLA_ROPE]
    kr_rot = jnp.concatenate([-kr[:, half:], kr[:, :half]], axis=1)
    w_packed = jnp.concatenate([
        w_in[:, :o_kr],
        zeros(MLA_NOPE), kr, kr_rot,
        w_in[:, o_sb:],
    ], axis=1).astype(BF16)

    r = w_uq.shape[0]
    wq = w_uq.reshape(r, MLA_HEADS, MLA_NOPE + MLA_ROPE)
    rope = wq[:, :, MLA_NOPE:]
    rope_rot = jnp.concatenate([-rope[:, :, half:], rope[:, :, :half]], axis=2)
    wuq = jnp.concatenate([wq, rope_rot], axis=2).reshape(r, MLA_HEADS * LANES).astype(BF16)

    rk = w_ukv.shape[0]
    wkv = w_ukv.reshape(rk, MLA_HEADS, MLA_NOPE + MLA_V)
    wk = jnp.concatenate([wkv[:, :, :MLA_NOPE], jnp.zeros((rk, MLA_HEADS, LANES - MLA_NOPE), f)], axis=2)
    wk = wk.reshape(rk, MLA_HEADS * LANES).astype(BF16)
    wv = jnp.concatenate([wkv[:, :, MLA_NOPE:], jnp.zeros((rk, MLA_HEADS, LANES - MLA_V), f)], axis=2)
    wv = wv.reshape(rk, MLA_HEADS * LANES).astype(BF16)
    return w_packed, wuq, wk, wv


def _full(shape):
    n = len(shape)
    return pl.BlockSpec(shape, lambda *_: (0,) * n, pipeline_mode=pl.Buffered(1))


def _in_proj(x2, pos_c, invf, g_mix, w_packed, b_gate, g_q, wuq, g_kv, wk, wv, tm, kb):
    t, d = x2.shape
    row = lambda w: pl.BlockSpec((tm, w), lambda i: (i, 0))
    pos_spec = pl.BlockSpec((tm // _POS_GROUPS, LANES), lambda i: (i, 0))
    hw = MLA_HEADS * LANES
    out_widths = [hw, hw, None, _SBW, _SBW, _SBW, d, d]
    vt_shape = jax.ShapeDtypeStruct((t // kb, MLA_HEADS * MLA_VT_ROWS, kb), BF16)
    vt_spec = pl.BlockSpec((tm // kb, MLA_HEADS * MLA_VT_ROWS, kb), lambda i: (i, 0, 0))
    return pl.pallas_call(
        functools.partial(_in_proj_kernel, d_model=d),
        out_shape=[vt_shape if w is None else jax.ShapeDtypeStruct((t, w), BF16) for w in out_widths],
        grid=(t // tm,),
        in_specs=[row(d), pos_spec, _full(invf.shape), _full(g_mix.shape), _full(w_packed.shape),
                  _full(b_gate.shape), _full(g_q.shape), _full(wuq.shape),
                  _full(g_kv.shape), _full(wk.shape), _full(wv.shape)],
        out_specs=[vt_spec if w is None else row(w) for w in out_widths],
        compiler_params=pltpu.CompilerParams(dimension_semantics=("parallel",), vmem_limit_bytes=VMEM_LIMIT),
        name="in_proj",
    )(x2, pos_c, invf, g_mix, w_packed, b_gate, g_q, wuq, g_kv, wk, wv)


def _work_items(blocks, n_chunks, rc, kb, strict):
    items = []
    for boff, diag in blocks:
        for j in range(2):
            for r in range(n_chunks):
                if diag is None:
                    items.append((boff, diag, j, r, False))
                    continue
                lo_row, hi_row = r * rc, (r + 1) * rc - 1
                lo_col, hi_col = diag * kb, (diag + 1) * kb - 1
                if (lo_col >= hi_row) if strict else (lo_col > hi_row):
                    continue
                fully_visible = (hi_col < lo_row) if strict else (hi_col <= lo_row)
                items.append((boff, diag, j, r, not fully_visible))
    return items


def _tile_rows(x, n):
    return jnp.concatenate([x] * n, axis=0)


def _mla_kernel(q_ref, k_ref, vt_ref, o_ref, m_sc, acc_sc, *, tq, kb, rc, look, blocks_per_body):
    qi = pl.program_id(2)
    nd = tq // kb
    nrc = tq // rc
    key_i = lax.broadcasted_iota(jnp.int32, (kb, rc), 0)
    qry_i = lax.broadcasted_iota(jnp.int32, (kb, rc), 1)
    m_sc[...] = jnp.full(m_sc.shape, NEG_BIG, F32)
    acc_sc[...] = jnp.zeros(acc_sc.shape, F32)

    def region(base_blk, work):
        def scores(i, _):
            boff, _diag, j, r, _mk = work[i]
            hs = slice(j * LANES, (j + 1) * LANES)
            keys = pl.ds(pl.multiple_of((base_blk + boff) * kb, kb), kb)
            return _dot_nt(k_ref[keys, hs], q_ref[r * rc:(r + 1) * rc, hs])

        def finish(i, s):
            boff, diag, j, r, mk = work[i]
            qs = slice(r * rc, (r + 1) * rc)
            if mk:
                s = jnp.where(key_i + (diag * kb - r * rc) <= qry_i, s, NEG_BIG)
            m_prev = m_sc[j, :, qs]
            m_new = jnp.maximum(m_prev, jnp.max(s, axis=0, keepdims=True))
            alpha = jnp.exp2(m_prev - m_new)
            p = jnp.exp2(s - _tile_rows(m_new, kb // SUBLANES))
            v_t = vt_ref[base_blk + boff, j * MLA_VT_ROWS:(j + 1) * MLA_VT_ROWS, :]
            pv = _dot(v_t, p.astype(BF16))
            acc_sc[j, :, qs] = _tile_rows(alpha, MLA_VT_ROWS // SUBLANES) * acc_sc[j, :, qs] + pv
            m_sc[j, :, qs] = m_new

        _pipelined(len(work), [scores, finish], [look])

    def full_work(n_blocks):
        return _work_items([(o, None) for o in range(n_blocks)], nrc, rc, kb, strict=False)

    n_full = qi * nd
    n_big = n_full // blocks_per_body

    def body(i, carry):
        region(i * blocks_per_body, full_work(blocks_per_body))
        return carry

    lax.fori_loop(0, n_big, body, 0)
    rem = n_full - n_big * blocks_per_body
    for tiles in range(1, blocks_per_body // nd):
        @pl.when(rem == tiles * nd)
        def _():
            region(n_big * blocks_per_body, full_work(tiles * nd))

    region(n_full, _work_items([(d, d) for d in range(nd)], nrc, rc, kb, strict=False))

    for j in range(2):
        denom = _tile_rows(acc_sc[j, MLA_V:MLA_V + SUBLANES, :], MLA_V // SUBLANES)
        o_ref[j * MLA_V:(j + 1) * MLA_V, :] = (acc_sc[j, :MLA_V, :] / denom).astype(o_ref.dtype)


def _mla_attn(q, k, vt, tq, kb, rc, look, blocks_per_body):
    b, s, _ = q.shape
    hp = MLA_HEADS // 2
    assert blocks_per_body % (tq // kb) == 0 or (tq // kb) % blocks_per_body == 0
    return pl.pallas_call(
        functools.partial(_mla_kernel, tq=tq, kb=kb, rc=rc, look=look, blocks_per_body=blocks_per_body),
        out_shape=jax.ShapeDtypeStruct((b, MLA_HEADS * MLA_V, s), BF16),
        grid=(b, hp, s // tq),
        in_specs=[pl.BlockSpec((None, tq, 2 * LANES), lambda bi, h, i: (bi, i, h)),
                  pl.BlockSpec((None, s, 2 * LANES), lambda bi, h, i: (bi, 0, h)),
                  pl.BlockSpec((None, s // kb, 2 * MLA_VT_ROWS, kb), lambda bi, h, i: (bi, 0, h, 0))],
        out_specs=pl.BlockSpec((None, 2 * MLA_V, tq), lambda bi, h, i: (bi, h, i)),
        scratch_shapes=[pltpu.VMEM((2, SUBLANES, tq), F32), pltpu.VMEM((2, MLA_VT_ROWS, tq), F32)],
        compiler_params=pltpu.CompilerParams(dimension_semantics=("parallel", "parallel", "parallel"),
                                             vmem_limit_bytes=VMEM_LIMIT),
        name="mla_attn",
    )(q, k, vt)


def _sb_kernel(q_ref, k_ref, v_ref, tri_ref, o_ref, qz_sc, r_sc, acc_sc, *, tq, kb, rc, lag_a, lag_b):
    qi = pl.program_id(2)
    nd = tq // kb
    nrc = tq // rc
    rows = lax.broadcasted_iota(jnp.int32, (rc, kb), 0)
    cols = lax.broadcasted_iota(jnp.int32, (rc, kb), 1)
    lane = lax.broadcasted_iota(jnp.int32, (tq, LANES), 1)
    q_both = q_ref[...]
    for j in range(2):
        in_head = (lane >= j * SB_HEAD_DIM) & (lane < (j + 1) * SB_HEAD_DIM)
        qz_sc[j] = jnp.where(in_head, q_both, jnp.zeros_like(q_both))
    r_sc[...] = jnp.zeros(r_sc.shape, F32)
    acc_sc[...] = jnp.zeros(acc_sc.shape, F32)

    def region(base_blk, work, tracked=(None,)):
        r_min = [None] * len(tracked)

        def kv_rows(i):
            return pl.ds(pl.multiple_of((base_blk + work[i][0]) * kb, kb), kb)

        def mask_of(i):
            _boff, diag, _j, r, _mk = work[i]
            return cols + (diag * kb - r * rc) < rows

        def scores(i, _):
            _boff, _diag, j, r, _mk = work[i]
            return _dot_nt(qz_sc[j, r * rc:(r + 1) * rc, :], k_ref[kv_rows(i), :])

        def suffix(i, z):
            _boff, _diag, j, r, mk = work[i]
            rs = slice(r * rc, (r + 1) * rc)
            sp = jnp.where(z > SOFTPLUS_LINEAR_ABOVE, z, jnp.log(1.0 + jnp.exp2(z)) * LOG2E)
            if mk:
                sp = jnp.where(mask_of(i), sp, 0.0)
            c = _dot(sp.astype(BF16), tri_ref[...])
            r_prev = r_sc[j, rs, :]
            r_new = r_prev + jnp.sum(sp, axis=-1, keepdims=True)
            r_sc[j, rs, :] = r_new
            wanted = [n for n, members in enumerate(tracked) if members is None or i in members]
            if wanted:
                parts = [r_new[t8 * SUBLANES:(t8 + 1) * SUBLANES, :] for t8 in range(rc // SUBLANES)]
                while len(parts) > 1:
                    parts = [jnp.minimum(a, b) for a, b in zip(parts[0::2], parts[1::2])]
                for n in wanted:
                    r_min[n] = parts[0] if r_min[n] is None else jnp.minimum(r_min[n], parts[0])
            return z, c, r_prev

        def weights(i, zcr):
            _boff, _diag, j, r, mk = work[i]
            rs = slice(r * rc, (r + 1) * rc)
            z, c, r_prev = zcr
            a = jnp.exp2(z - c - jnp.concatenate([r_prev] * (kb // LANES), axis=1))
            if mk:
                a = jnp.where(mask_of(i), a, 0.0)
            acc_sc[j, rs, :] += _dot(a.astype(BF16), v_ref[kv_rows(i), :])

        _pipelined(len(work), [scores, suffix, weights], [lag_a, lag_b])
        return r_min

    def still_open(low):
        return (jnp.min(low) <= SB_UNDERFLOW_LOG2).astype(jnp.int32)

    diag_items = _work_items([(d, d) for d in reversed(range(nd))], nrc, rc, kb, strict=True)
    own_block = lambda r: ((r + 1) * rc - 1) // kb
    near_items = [it for it in diag_items if it[1] >= own_block(it[3]) - 1]
    far_items = [it for it in diag_items if it[1] < own_block(it[3]) - 1]
    far_chunks = {(it[2], it[3]) for it in far_items}
    n_top = kb // rc
    last_near = [i for i, it in enumerate(near_items) if it[1] == max(own_block(it[3]) - 1, 0)]
    far_deciding = {i for i in last_near if (near_items[i][2], near_items[i][3]) in far_chunks}
    lower_deciding = {i for i in last_near if near_items[i][3] >= n_top}
    far_low, lower_low = region(qi * nd, near_items, tracked=(far_deciding, lower_deciding))
    if far_items:
        @pl.when(still_open(far_low) > 0)
        def _():
            region(qi * nd, far_items, tracked=())

    n_full = qi * nd
    full_work = _work_items([(0, None)], nrc, rc, kb, strict=True)
    top_work = [it for it in full_work if it[3] < n_top]
    lower_work = [it for it in full_work if it[3] >= n_top]

    def cond(carry):
        i, top_open, lower_open = carry
        return jnp.logical_and(i < n_full, (top_open + lower_open) > 0)

    def body(carry):
        i, _, lower_open = carry
        blk = n_full - 1 - i
        top_open = still_open(region(blk, top_work)[0])
        lower_open = lax.cond(lower_open > 0, lambda: still_open(region(blk, lower_work)[0]),
                              lambda: jnp.int32(0))
        return i + 1, top_open, lower_open

    lower_open0 = still_open(lower_low) if lower_work else jnp.int32(0)
    lax.while_loop(cond, body, (jnp.int32(0), jnp.int32(1), lower_open0))
    o_ref[...] = jnp.where(lane < SB_HEAD_DIM, acc_sc[0], acc_sc[1]).astype(o_ref.dtype)


def _sb_attn(q, k, v, tq, kb, rc, lag_a, lag_b):
    b, s, _ = q.shape
    hp = SB_HEADS // 2
    idx = jnp.arange(kb)
    tri = (idx[:, None] >= idx[None, :]).astype(BF16)
    return pl.pallas_call(
        functools.partial(_sb_kernel, tq=tq, kb=kb, rc=rc, lag_a=lag_a, lag_b=lag_b),
        out_shape=jax.ShapeDtypeStruct((b, s, _SBW), BF16),
        grid=(b, hp, s // tq),
        in_specs=[pl.BlockSpec((None, tq, LANES), lambda bi, h, i: (bi, i, h)),
                  pl.BlockSpec((None, s, LANES), lambda bi, h, i: (bi, 0, h)),
                  pl.BlockSpec((None, s, LANES), lambda bi, h, i: (bi, 0, h)),
                  pl.BlockSpec((kb, kb), lambda bi, h, i: (0, 0))],
        out_specs=pl.BlockSpec((None, tq, LANES), lambda bi, h, i: (bi, i, h)),
        scratch_shapes=[pltpu.VMEM((2, tq, LANES), BF16), pltpu.VMEM((2, tq, LANES), F32),
                        pltpu.VMEM((2, tq, LANES), F32)],
        compiler_params=pltpu.CompilerParams(dimension_semantics=("parallel", "parallel", "parallel"),
                                             vmem_limit_bytes=VMEM_LIMIT),
        name="sb_attn",
    )(q, k, v, tri)


def _mem_kv_kernel(mem_ref, g_ref, w_ref, kv_ref):
    mn = _rms(mem_ref[...], g_ref[...]).astype(BF16)
    kv_ref[...] = _dot(mn, w_ref[...]).astype(BF16)


def _mem_kv(mem2, g_mem, w_xkv):
    rows, _ = mem2.shape
    n = w_xkv.shape[1]
    return pl.pallas_call(
        _mem_kv_kernel,
        out_shape=jax.ShapeDtypeStruct((rows, n), BF16),
        grid=(1,),
        in_specs=[_full(mem2.shape), _full(g_mem.shape), _full(w_xkv.shape)],
        out_specs=_full((rows, n)),
        compiler_params=pltpu.CompilerParams(vmem_limit_bytes=VMEM_LIMIT),
        name="mem_kv",
    )(mem2, g_mem, w_xkv)


def _merge_kernel(x_ref, oa_ref, ob_ref, ga_ref, gb_ref, wa_ref, wb_ref, wo_ref, gx_ref, wxq_ref, kv_ref,
                  wxo_ref, y_ref):
    pa = lax.dot_general(oa_ref[...], wa_ref[...], (((0,), (0,)), ((), ())), preferred_element_type=F32)
    merged = ga_ref[...].astype(F32) * pa + gb_ref[...].astype(F32) * _dot(ob_ref[...], wb_ref[...])
    x1 = x_ref[...] + _dot(merged.astype(BF16), wo_ref[...])

    hx = _rms(x1, gx_ref[...]).astype(BF16)
    xw = X_HEADS * X_HEAD_DIM
    xq = (_dot(hx, wxq_ref[...]) * (LOG2E / math.sqrt(X_HEAD_DIM))).astype(BF16)
    heads = []
    for hd in range(X_HEADS):
        sl = slice(hd * X_HEAD_DIM, (hd + 1) * X_HEAD_DIM)
        kh = kv_ref[:, sl]
        vh = kv_ref[:, xw + hd * X_HEAD_DIM: xw + (hd + 1) * X_HEAD_DIM]
        s = _dot_nt(xq[:, sl], kh)
        p = jnp.exp2(s - jnp.max(s, axis=-1, keepdims=True))
        l = jnp.sum(p, axis=-1, keepdims=True)
        heads.append((_dot(p.astype(BF16), vh) / l).astype(BF16))
    xo = jnp.concatenate(heads, axis=1)
    y_ref[...] = x1 + _dot(xo, wxo_ref[...])


def _merge(x2, oa_t, ob, ga, gb, wa, wb, wo, g_x, wxq, kv, wxo, tm, seq, m_len):
    t, d = x2.shape
    per_b = seq // tm
    row = lambda w: pl.BlockSpec((tm, w), lambda i: (i, 0))
    return pl.pallas_call(
        _merge_kernel,
        out_shape=jax.ShapeDtypeStruct((t, d), F32),
        grid=(t // tm,),
        in_specs=[row(d), pl.BlockSpec((None, oa_t.shape[1], tm), lambda i: (i // per_b, 0, i % per_b)),
                  row(ob.shape[1]), row(d), row(d),
                  _full(wa.shape), _full(wb.shape), _full(wo.shape), _full(g_x.shape), _full(wxq.shape),
                  pl.BlockSpec((m_len, kv.shape[1]), lambda i: (i // per_b, 0)),
                  _full(wxo.shape)],
        out_specs=row(d),
        compiler_params=pltpu.CompilerParams(dimension_semantics=("parallel",), vmem_limit_bytes=VMEM_LIMIT),
        name="merge_xattn",
    )(x2, oa_t, ob, ga, gb, wa, wb, wo, g_x, wxq, kv, wxo)


def _ffn_kernel(x_ref, g_ref, wg_ref, wu_ref, wd_ref, gf_ref, y_ref, *, n_chunks):
    x = x_ref[...]
    hf = _rms(x, g_ref[...]).astype(BF16)
    d_ff = wg_ref.shape[1]
    cw = d_ff // n_chunks
    y = x
    for c in range(n_chunks):
        sl = slice(c * cw, (c + 1) * cw)
        g = _dot(hf, wg_ref[:, sl])
        u = _dot(hf, wu_ref[:, sl])
        act = (g * jax.nn.sigmoid(g) * u).astype(BF16)
        y = y + _dot(act, wd_ref[sl, :])
    y_ref[...] = _rms(y, gf_ref[...])


def _ffn(x2, g_ffn, wg, wu, wd, g_final, tm, n_chunks):
    t, d = x2.shape
    row = pl.BlockSpec((tm, d), lambda i: (i, 0))
    return pl.pallas_call(
        functools.partial(_ffn_kernel, n_chunks=n_chunks),
        out_shape=jax.ShapeDtypeStruct((t, d), F32),
        grid=(t // tm,),
        in_specs=[row, _full(g_ffn.shape), _full(wg.shape), _full(wu.shape), _full(wd.shape), _full(g_final.shape)],
        out_specs=row,
        compiler_params=pltpu.CompilerParams(dimension_semantics=("parallel",), vmem_limit_bytes=VMEM_LIMIT),
        name="ffn",
    )(x2, g_ffn, wg, wu, wd, g_final)


def kernel(x, mem, positions, g_mix, w_in, b_gate, g_q_lat, w_uq, g_kv_lat, w_ukv, w_a_proj, w_b_proj, w_o,
           g_x, g_mem, w_xq, w_xkv, w_xo, g_ffn, w_gate, w_up, w_down, g_final):
    b, s, d = x.shape
    m_len = mem.shape[1]
    assert g_mix.shape[0] == 1, "the final RMSNorm is fused into the (single) layer's FFN kernel"
    tm = min(1024, s)
    tm_in = min(1024, s)
    tq = min(1024, s)
    kb = min(256, s)
    rc = min(128, s)
    t = b * s

    inv_freq = ROPE_THETA ** (-jnp.arange(0, MLA_ROPE, 2, dtype=F32) / MLA_ROPE)
    invf = jnp.tile(inv_freq, 2 * _POS_GROUPS)[None, :]
    pos_c = positions.reshape(t // tm_in, _POS_GROUPS, tm_in // _POS_GROUPS).transpose(0, 2, 1)
    pos_c = jnp.repeat(pos_c.reshape(t // _POS_GROUPS, _POS_GROUPS), MLA_ROPE, axis=1)
    x2 = x.reshape(t, d)
    mem2 = mem.reshape(b * m_len, d)

    w_packed, wuq, wk, wv = _pack_in_weights(w_in[0], w_uq[0], w_ukv[0], d)
    q, k, vt, sbq, sbk, sbv, ga, gb = _in_proj(
        x2, pos_c, invf, g_mix[0][None, :], w_packed, b_gate[0], g_q_lat[0][None, :], wuq,
        g_kv_lat[0][None, :], wk, wv, tm_in, kb)
    o_a_t = _mla_attn(q.reshape(b, s, -1), k.reshape(b, s, -1), vt.reshape(b, s // kb, -1, kb), min(2048, s), kb,
                      kb, look=6, blocks_per_body=4)
    o_b = _sb_attn(sbq.reshape(b, s, -1), sbk.reshape(b, s, -1), sbv.reshape(b, s, -1), tq, kb, rc,
                   lag_a=3, lag_b=3)
    kv = _mem_kv(mem2, g_mem[0][None, :], w_xkv[0].astype(BF16))
    x2 = _merge(x2, o_a_t, o_b.reshape(t, -1), ga, gb,
                w_a_proj[0].astype(BF16), w_b_proj[0].astype(BF16), w_o[0].astype(BF16),
                g_x[0][None, :], w_xq[0].astype(BF16), kv, w_xo[0].astype(BF16), tm, s, m_len)
    x2 = _ffn(x2, g_ffn[0][None, :], w_gate[0].astype(BF16), w_up[0].astype(BF16), w_down[0].astype(BF16),
              g_final[None, :], tm, 1)
    return x2.reshape(b, s, d)
```

```python
import functools
import math

import jax
import jax.numpy as jnp
from jax import lax
from jax.experimental import pallas as pl
from jax.experimental.pallas import tpu as pltpu

F32 = jnp.float32
BF16 = jnp.bfloat16

EPS = 1e-6
ROPE_THETA = 10000.0
LOG2E = math.log2(math.e)

MLA_HEADS = 8
MLA_Q_RANK = 256
MLA_KV_RANK = 128
MLA_NOPE = 64
MLA_ROPE = 32
MLA_V = 64
SB_HEADS = 8
SB_HEAD_DIM = 64
X_HEADS = 4
X_HEAD_DIM = 128
LANES = 128
SUBLANES = 8
BF16_ROWS = 16
MLA_VT_ROWS = MLA_V + BF16_ROWS
NEG_BIG = -1e30
SB_UNDERFLOW_LOG2 = 160.0
SOFTPLUS_LINEAR_ABOVE = 64.0

VMEM_LIMIT = 56 * 1024 * 1024


def _rms(x, g):
    ms = jnp.mean(x * x, axis=-1, keepdims=True)
    return x * lax.rsqrt(ms + EPS) * g


def _dot(a, b):
    return jnp.dot(a, b, preferred_element_type=F32)


def _dot_nt(a, b):
    return lax.dot_general(a, b, (((1,), (1,)), ((), ())), preferred_element_type=F32)


def _pipelined(n_items, stages, lags):
    offs = [0]
    for lag in lags:
        offs.append(offs[-1] + lag)
    vals = [dict() for _ in stages]
    for tick in range(n_items + offs[-1]):
        for k, stage in enumerate(stages):
            i = tick - offs[k]
            if 0 <= i < n_items:
                prev = vals[k - 1].pop(i) if k > 0 else None
                vals[k][i] = stage(i, prev)


_C_CQ = 0
_C_CKV = _C_CQ + MLA_Q_RANK
_C_KRX = _C_CKV + MLA_KV_RANK
_SBW = SB_HEADS * SB_HEAD_DIM
_T_SBQ = 0
_T_SBK = _T_SBQ + _SBW
_T_SBV = _T_SBK + _SBW
_T_GA = _T_SBV + _SBW
_ROPE_END = MLA_NOPE + MLA_ROPE
_POS_GROUPS = LANES // MLA_ROPE


def _in_proj_kernel(x_ref, pos_ref, invf_ref, gmix_ref, whead_ref, wtail_ref, bg_ref, gq_ref, wuq_ref,
                    gkv_ref, wk_ref, wv_ref,
                    q_ref, k_ref, vt_ref, sbq_ref, sbk_ref, sbv_ref, ga_ref, gb_ref, *, d_model):
    x = x_ref[...]
    h = _rms(x, gmix_ref[...]).astype(BF16)

    def proj(c0, width, w_ref=wtail_ref):
        return _dot(h, w_ref[:, c0:c0 + width])

    cq_raw = proj(_C_CQ, MLA_Q_RANK, whead_ref)
    ckv_krx = proj(_C_CKV, MLA_KV_RANK + LANES, whead_ref)
    sbq_ref[...] = (proj(_T_SBQ, _SBW) * (LOG2E / math.sqrt(SB_HEAD_DIM))).astype(BF16)
    sbk_ref[...] = proj(_T_SBK, _SBW).astype(BF16)

    cq = _rms(cq_raw, gq_ref[...]).astype(BF16)
    ckv = _rms(ckv_krx[:, :MLA_KV_RANK], gkv_ref[...]).astype(BF16)
    qt = _dot(cq, wuq_ref[...])
    kn = _dot(ckv, wk_ref[...])
    vv = _dot(ckv, wv_ref[...])
    gate_a = proj(_T_GA, d_model)
    gate_b = proj(_T_GA + d_model, d_model)
    sbv_ref[...] = proj(_T_SBV, _SBW).astype(BF16)

    lane = lax.broadcasted_iota(jnp.int32, (1, LANES), 1)
    ang = pos_ref[...].astype(F32) * invf_ref[...]
    cos_c = jnp.cos(ang)
    sin_c = jnp.sin(ang)
    is_rope = (lane >= MLA_NOPE) & (lane < _ROPE_END)
    cos_parts, sin_parts = [], []
    for g in range(_POS_GROUPS):
        shift = (MLA_NOPE - g * MLA_ROPE) % LANES
        cos_g = cos_c if shift == 0 else pltpu.roll(cos_c, shift, axis=1)
        sin_g = sin_c if shift == 0 else pltpu.roll(sin_c, shift, axis=1)
        cos_parts.append(jnp.where(is_rope, cos_g, jnp.where(lane < MLA_NOPE, 1.0, 0.0)))
        sin_parts.append(jnp.where(is_rope, sin_g, 0.0))
    cos = jnp.concatenate(cos_parts, axis=0)
    sin = jnp.concatenate(sin_parts, axis=0)
    partner_shift = LANES - MLA_ROPE

    def rotary(t, c, s):
        return t * c + pltpu.roll(t, partner_shift, axis=1) * s

    q_scale = LOG2E / math.sqrt(MLA_NOPE + MLA_ROPE)
    cos_q = cos * q_scale
    sin_q = sin * q_scale
    for hd in range(MLA_HEADS):
        sl = slice(hd * LANES, (hd + 1) * LANES)
        q_ref[:, sl] = rotary(qt[:, sl], cos_q, sin_q).astype(BF16)

    krope = rotary(ckv_krx[:, MLA_KV_RANK:], cos, sin)
    for hd in range(MLA_HEADS):
        sl = slice(hd * LANES, (hd + 1) * LANES)
        k_ref[:, sl] = (kn[:, sl] + krope).astype(BF16)

    v_lane = lax.broadcasted_iota(jnp.int32, (1, MLA_HEADS * LANES), 1)
    ones_cols = jnp.where(v_lane % LANES >= MLA_V, 1.0, 0.0).astype(F32)
    vv = vv + ones_cols
    n_kblk, _, kb = vt_ref.shape
    for c in range(n_kblk):
        for hd in range(MLA_HEADS):
            v_t = vv[c * kb:(c + 1) * kb, hd * LANES:(hd + 1) * LANES].T
            vt_ref[c, hd * MLA_VT_ROWS:(hd + 1) * MLA_VT_ROWS, :] = v_t[:MLA_VT_ROWS].astype(BF16)

    ga_ref[...] = jax.nn.sigmoid(gate_a + bg_ref[0:1, :]).astype(BF16)
    gb_ref[...] = jax.nn.sigmoid(gate_b + bg_ref[1:2, :]).astype(BF16)


def _pack_in_weights(w_in, w_uq, w_ukv, d_model):
    f = w_in.dtype
    zeros = lambda n: jnp.zeros((d_model, n), f)
    o_kr = MLA_Q_RANK + MLA_KV_RANK
    o_sb = o_kr + MLA_ROPE
    half = MLA_ROPE // 2
    assert LANES - MLA_NOPE - MLA_ROPE == MLA_ROPE
    kr = w_in[:, o_kr:o_kr + MLA_ROPE]
    kr_rot = jnp.concatenate([-kr[:, half:], kr[:, :half]], axis=1)
    w_head = jnp.concatenate([w_in[:, :o_kr], zeros(MLA_NOPE), kr, kr_rot], axis=1).astype(BF16)
    w_tail = w_in[:, o_sb:].astype(BF16)

    r = w_uq.shape[0]
    wq = w_uq.reshape(r, MLA_HEADS, MLA_NOPE + MLA_ROPE)
    rope = wq[:, :, MLA_NOPE:]
    rope_rot = jnp.concatenate([-rope[:, :, half:], rope[:, :, :half]], axis=2)
    wuq = jnp.concatenate([wq, rope_rot], axis=2).reshape(r, MLA_HEADS * LANES).astype(BF16)

    rk = w_ukv.shape[0]
    wkv = w_ukv.reshape(rk, MLA_HEADS, MLA_NOPE + MLA_V)
    wk = jnp.concatenate([wkv[:, :, :MLA_NOPE], jnp.zeros((rk, MLA_HEADS, LANES - MLA_NOPE), f)], axis=2)
    wk = wk.reshape(rk, MLA_HEADS * LANES).astype(BF16)
    wv = jnp.concatenate([wkv[:, :, MLA_NOPE:], jnp.zeros((rk, MLA_HEADS, LANES - MLA_V), f)], axis=2)
    wv = wv.reshape(rk, MLA_HEADS * LANES).astype(BF16)
    return w_head, w_tail, wuq, wk, wv


def _full(shape):
    n = len(shape)
    return pl.BlockSpec(shape, lambda *_: (0,) * n, pipeline_mode=pl.Buffered(1))


def _in_proj(x2, pos_c, invf, g_mix, w_head, w_tail, b_gate, g_q, wuq, g_kv, wk, wv, tm, kb):
    t, d = x2.shape
    row = lambda w: pl.BlockSpec((tm, w), lambda i: (i, 0))
    pos_spec = pl.BlockSpec((tm // _POS_GROUPS, LANES), lambda i: (i, 0))
    hw = MLA_HEADS * LANES
    out_widths = [hw, hw, None, _SBW, _SBW, _SBW, d, d]
    vt_shape = jax.ShapeDtypeStruct((t // kb, MLA_HEADS * MLA_VT_ROWS, kb), BF16)
    vt_spec = pl.BlockSpec((tm // kb, MLA_HEADS * MLA_VT_ROWS, kb), lambda i: (i, 0, 0))
    return pl.pallas_call(
        functools.partial(_in_proj_kernel, d_model=d),
        out_shape=[vt_shape if w is None else jax.ShapeDtypeStruct((t, w), BF16) for w in out_widths],
        grid=(t // tm,),
        in_specs=[row(d), pos_spec, _full(invf.shape), _full(g_mix.shape), _full(w_head.shape), _full(w_tail.shape),
                  _full(b_gate.shape), _full(g_q.shape), _full(wuq.shape),
                  _full(g_kv.shape), _full(wk.shape), _full(wv.shape)],
        out_specs=[vt_spec if w is None else row(w) for w in out_widths],
        compiler_params=pltpu.CompilerParams(dimension_semantics=("parallel",), vmem_limit_bytes=VMEM_LIMIT),
        name="in_proj",
    )(x2, pos_c, invf, g_mix, w_head, w_tail, b_gate, g_q, wuq, g_kv, wk, wv)


def _work_items(blocks, n_chunks, rc, kb, strict):
    items = []
    for boff, diag in blocks:
        for j in range(2):
            for r in range(n_chunks):
                if diag is None:
                    items.append((boff, diag, j, r, False))
                    continue
                lo_row, hi_row = r * rc, (r + 1) * rc - 1
                lo_col, hi_col = diag * kb, (diag + 1) * kb - 1
                if (lo_col >= hi_row) if strict else (lo_col > hi_row):
                    continue
                fully_visible = (hi_col < lo_row) if strict else (hi_col <= lo_row)
                items.append((boff, diag, j, r, not fully_visible))
    return items


def _tile_rows(x, n):
    return jnp.concatenate([x] * n, axis=0)


def _mla_kernel(q_ref, k_ref, vt_ref, o_ref, m_sc, acc_sc, *, tq, kb, rc, look, blocks_per_body):
    qi = pl.program_id(2)
    nd = tq // kb
    nrc = tq // rc
    key_i = lax.broadcasted_iota(jnp.int32, (kb, rc), 0)
    qry_i = lax.broadcasted_iota(jnp.int32, (kb, rc), 1)
    m_sc[...] = jnp.full(m_sc.shape, NEG_BIG, F32)
    acc_sc[...] = jnp.zeros(acc_sc.shape, F32)

    def region(base_blk, work):
        def scores(i, _):
            boff, _diag, j, r, _mk = work[i]
            hs = slice(j * LANES, (j + 1) * LANES)
            keys = pl.ds(pl.multiple_of((base_blk + boff) * kb, kb), kb)
            return _dot_nt(k_ref[keys, hs], q_ref[r * rc:(r + 1) * rc, hs])

        def finish(i, s):
            boff, diag, j, r, mk = work[i]
            qs = slice(r * rc, (r + 1) * rc)
            if mk:
                s = jnp.where(key_i + (diag * kb - r * rc) <= qry_i, s, NEG_BIG)
            m_prev = m_sc[j, :, qs]
            m_new = jnp.maximum(m_prev, jnp.max(s, axis=0, keepdims=True))
            alpha = jnp.exp2(m_prev - m_new)
            p = jnp.exp2(s - _tile_rows(m_new, kb // SUBLANES))
            v_t = vt_ref[base_blk + boff, j * MLA_VT_ROWS:(j + 1) * MLA_VT_ROWS, :]
            pv = _dot(v_t, p.astype(BF16))
            acc_sc[j, :, qs] = _tile_rows(alpha, MLA_VT_ROWS // SUBLANES) * acc_sc[j, :, qs] + pv
            m_sc[j, :, qs] = m_new

        _pipelined(len(work), [scores, finish], [look])

    def full_work(n_blocks):
        return _work_items([(o, None) for o in range(n_blocks)], nrc, rc, kb, strict=False)

    n_full = qi * nd
    n_big = n_full // blocks_per_body

    def body(i, carry):
        region(i * blocks_per_body, full_work(blocks_per_body))
        return carry

    lax.fori_loop(0, n_big, body, 0)
    rem = n_full - n_big * blocks_per_body
    for tiles in range(1, blocks_per_body // nd):
        @pl.when(rem == tiles * nd)
        def _():
            region(n_big * blocks_per_body, full_work(tiles * nd))

    region(n_full, _work_items([(d, d) for d in range(nd)], nrc, rc, kb, strict=False))

    for j in range(2):
        denom = _tile_rows(acc_sc[j, MLA_V:MLA_V + SUBLANES, :], MLA_V // SUBLANES)
        o_ref[j * MLA_V:(j + 1) * MLA_V, :] = (acc_sc[j, :MLA_V, :] / denom).astype(o_ref.dtype)


def _mla_attn(q, k, vt, tq, kb, rc, look, blocks_per_body):
    b, s, _ = q.shape
    hp = MLA_HEADS // 2
    assert blocks_per_body % (tq // kb) == 0 or (tq // kb) % blocks_per_body == 0
    return pl.pallas_call(
        functools.partial(_mla_kernel, tq=tq, kb=kb, rc=rc, look=look, blocks_per_body=blocks_per_body),
        out_shape=jax.ShapeDtypeStruct((b, MLA_HEADS * MLA_V, s), BF16),
        grid=(b, hp, s // tq),
        in_specs=[pl.BlockSpec((None, tq, 2 * LANES), lambda bi, h, i: (bi, i, h)),
                  pl.BlockSpec((None, s, 2 * LANES), lambda bi, h, i: (bi, 0, h)),
                  pl.BlockSpec((None, s // kb, 2 * MLA_VT_ROWS, kb), lambda bi, h, i: (bi, 0, h, 0))],
        out_specs=pl.BlockSpec((None, 2 * MLA_V, tq), lambda bi, h, i: (bi, h, i)),
        scratch_shapes=[pltpu.VMEM((2, SUBLANES, tq), F32), pltpu.VMEM((2, MLA_VT_ROWS, tq), F32)],
        compiler_params=pltpu.CompilerParams(dimension_semantics=("parallel", "parallel", "parallel"),
                                             vmem_limit_bytes=VMEM_LIMIT),
        name="mla_attn",
    )(q, k, vt)


def _sb_kernel(q_ref, k_ref, v_ref, tri_ref, o_ref, qz_sc, r_sc, acc_sc, *, tq, kb, rc, lag_a, lag_b):
    qi = pl.program_id(2)
    nd = tq // kb
    nrc = tq // rc
    rows = lax.broadcasted_iota(jnp.int32, (rc, kb), 0)
    cols = lax.broadcasted_iota(jnp.int32, (rc, kb), 1)
    lane = lax.broadcasted_iota(jnp.int32, (tq, LANES), 1)
    q_both = q_ref[...]
    for j in range(2):
        in_head = (lane >= j * SB_HEAD_DIM) & (lane < (j + 1) * SB_HEAD_DIM)
        qz_sc[j] = jnp.where(in_head, q_both, jnp.zeros_like(q_both))
    r_sc[...] = jnp.zeros(r_sc.shape, F32)
    acc_sc[...] = jnp.zeros(acc_sc.shape, F32)

    def region(base_blk, work, tracked=(None,)):
        r_min = [None] * len(tracked)

        def kv_rows(i):
            return pl.ds(pl.multiple_of((base_blk + work[i][0]) * kb, kb), kb)

        def mask_of(i):
            _boff, diag, _j, r, _mk = work[i]
            return cols + (diag * kb - r * rc) < rows

        def scores(i, _):
            _boff, _diag, j, r, _mk = work[i]
            return _dot_nt(qz_sc[j, r * rc:(r + 1) * rc, :], k_ref[kv_rows(i), :])

        def suffix(i, z):
            _boff, _diag, j, r, mk = work[i]
            rs = slice(r * rc, (r + 1) * rc)
            sp = jnp.where(z > SOFTPLUS_LINEAR_ABOVE, z, jnp.log(1.0 + jnp.exp2(z)) * LOG2E)
            if mk:
                sp = jnp.where(mask_of(i), sp, 0.0)
            c = _dot(sp.astype(BF16), tri_ref[...])
            r_prev = r_sc[j, rs, :]
            r_new = r_prev + jnp.sum(sp, axis=-1, keepdims=True)
            r_sc[j, rs, :] = r_new
            wanted = [n for n, members in enumerate(tracked) if members is None or i in members]
            if wanted:
                parts = [r_new[t8 * SUBLANES:(t8 + 1) * SUBLANES, :] for t8 in range(rc // SUBLANES)]
                while len(parts) > 1:
                    parts = [jnp.minimum(a, b) for a, b in zip(parts[0::2], parts[1::2])]
                for n in wanted:
                    r_min[n] = parts[0] if r_min[n] is None else jnp.minimum(r_min[n], parts[0])
            return z, c, r_prev

        def weights(i, zcr):
            _boff, _diag, j, r, mk = work[i]
            rs = slice(r * rc, (r + 1) * rc)
            z, c, r_prev = zcr
            a = jnp.exp2(z - c - jnp.concatenate([r_prev] * (kb // LANES), axis=1))
            if mk:
                a = jnp.where(mask_of(i), a, 0.0)
            acc_sc[j, rs, :] += _dot(a.astype(BF16), v_ref[kv_rows(i), :])

        _pipelined(len(work), [scores, suffix, weights], [lag_a, lag_b])
        return r_min

    def still_open(low):
        return (jnp.min(low) <= SB_UNDERFLOW_LOG2).astype(jnp.int32)

    diag_items = _work_items([(d, d) for d in reversed(range(nd))], nrc, rc, kb, strict=True)
    own_block = lambda r: ((r + 1) * rc - 1) // kb
    near_items = [it for it in diag_items if it[1] >= own_block(it[3]) - 1]
    far_items = [it for it in diag_items if it[1] < own_block(it[3]) - 1]
    far_chunks = {(it[2], it[3]) for it in far_items}
    n_top = kb // rc
    last_near = [i for i, it in enumerate(near_items) if it[1] == max(own_block(it[3]) - 1, 0)]
    far_deciding = {i for i in last_near if (near_items[i][2], near_items[i][3]) in far_chunks}
    lower_deciding = {i for i in last_near if near_items[i][3] >= n_top}
    far_low, lower_low = region(qi * nd, near_items, tracked=(far_deciding, lower_deciding))
    if far_items:
        @pl.when(still_open(far_low) > 0)
        def _():
            region(qi * nd, far_items, tracked=())

    n_full = qi * nd
    full_work = _work_items([(0, None)], nrc, rc, kb, strict=True)
    top_work = [it for it in full_work if it[3] < n_top]
    lower_work = [it for it in full_work if it[3] >= n_top]

    def cond(carry):
        i, top_open, lower_open = carry
        return jnp.logical_and(i < n_full, (top_open + lower_open) > 0)

    def body(carry):
        i, _, lower_open = carry
        blk = n_full - 1 - i
        top_open = still_open(region(blk, top_work)[0])
        lower_open = lax.cond(lower_open > 0, lambda: still_open(region(blk, lower_work)[0]),
                              lambda: jnp.int32(0))
        return i + 1, top_open, lower_open

    lower_open0 = still_open(lower_low) if lower_work else jnp.int32(0)
    lax.while_loop(cond, body, (jnp.int32(0), jnp.int32(1), lower_open0))
    o_ref[...] = jnp.where(lane < SB_HEAD_DIM, acc_sc[0], acc_sc[1]).astype(o_ref.dtype)


def _sb_attn(q, k, v, tq, kb, rc, lag_a, lag_b):
    b, s, _ = q.shape
    hp = SB_HEADS // 2
    idx = jnp.arange(kb)
    tri = (idx[:, None] >= idx[None, :]).astype(BF16)
    return pl.pallas_call(
        functools.partial(_sb_kernel, tq=tq, kb=kb, rc=rc, lag_a=lag_a, lag_b=lag_b),
        out_shape=jax.ShapeDtypeStruct((b, s, _SBW), BF16),
        grid=(b, hp, s // tq),
        in_specs=[pl.BlockSpec((None, tq, LANES), lambda bi, h, i: (bi, i, h)),
                  pl.BlockSpec((None, s, LANES), lambda bi, h, i: (bi, 0, h)),
                  pl.BlockSpec((None, s, LANES), lambda bi, h, i: (bi, 0, h)),
                  pl.BlockSpec((kb, kb), lambda bi, h, i: (0, 0))],
        out_specs=pl.BlockSpec((None, tq, LANES), lambda bi, h, i: (bi, i, h)),
        scratch_shapes=[pltpu.VMEM((2, tq, LANES), BF16), pltpu.VMEM((2, tq, LANES), F32),
                        pltpu.VMEM((2, tq, LANES), F32)],
        compiler_params=pltpu.CompilerParams(dimension_semantics=("parallel", "parallel", "parallel"),
                                             vmem_limit_bytes=VMEM_LIMIT),
        name="sb_attn",
    )(q, k, v, tri)


def _mem_kv_kernel(mem_ref, g_ref, w_ref, kv_ref):
    mn = _rms(mem_ref[...], g_ref[...]).astype(BF16)
    kv_ref[...] = _dot(mn, w_ref[...]).astype(BF16)


def _mem_kv(mem2, g_mem, w_xkv):
    rows, _ = mem2.shape
    n = w_xkv.shape[1]
    return pl.pallas_call(
        _mem_kv_kernel,
        out_shape=jax.ShapeDtypeStruct((rows, n), BF16),
        grid=(1,),
        in_specs=[_full(mem2.shape), _full(g_mem.shape), _full(w_xkv.shape)],
        out_specs=_full((rows, n)),
        compiler_params=pltpu.CompilerParams(vmem_limit_bytes=VMEM_LIMIT),
        name="mem_kv",
    )(mem2, g_mem, w_xkv)


def _merge_kernel(x_ref, oa_ref, ob_ref, ga_ref, gb_ref, wa_ref, wb_ref, wo_ref, gx_ref, wxq_ref, kv_ref,
                  wxo_ref, y_ref):
    pa = lax.dot_general(oa_ref[...], wa_ref[...], (((0,), (0,)), ((), ())), preferred_element_type=F32)
    merged = ga_ref[...].astype(F32) * pa + gb_ref[...].astype(F32) * _dot(ob_ref[...], wb_ref[...])
    x1 = x_ref[...] + _dot(merged.astype(BF16), wo_ref[...])

    hx = _rms(x1, gx_ref[...]).astype(BF16)
    xw = X_HEADS * X_HEAD_DIM
    xq = (_dot(hx, wxq_ref[...]) * (LOG2E / math.sqrt(X_HEAD_DIM))).astype(BF16)
    heads = []
    for hd in range(X_HEADS):
        sl = slice(hd * X_HEAD_DIM, (hd + 1) * X_HEAD_DIM)
        kh = kv_ref[:, sl]
        vh = kv_ref[:, xw + hd * X_HEAD_DIM: xw + (hd + 1) * X_HEAD_DIM]
        s = _dot_nt(xq[:, sl], kh)
        p = jnp.exp2(s - jnp.max(s, axis=-1, keepdims=True))
        l = jnp.sum(p, axis=-1, keepdims=True)
        heads.append((_dot(p.astype(BF16), vh) / l).astype(BF16))
    xo = jnp.concatenate(heads, axis=1)
    y_ref[...] = x1 + _dot(xo, wxo_ref[...])


def _merge(x2, oa_t, ob, ga, gb, wa, wb, wo, g_x, wxq, kv, wxo, tm, seq, m_len):
    t, d = x2.shape
    per_b = seq // tm
    row = lambda w: pl.BlockSpec((tm, w), lambda i: (i, 0))
    return pl.pallas_call(
        _merge_kernel,
        out_shape=jax.ShapeDtypeStruct((t, d), F32),
        grid=(t // tm,),
        in_specs=[row(d), pl.BlockSpec((None, oa_t.shape[1], tm), lambda i: (i // per_b, 0, i % per_b)),
                  row(ob.shape[1]), row(d), row(d),
                  _full(wa.shape), _full(wb.shape), _full(wo.shape), _full(g_x.shape), _full(wxq.shape),
                  pl.BlockSpec((m_len, kv.shape[1]), lambda i: (i // per_b, 0)),
                  _full(wxo.shape)],
        out_specs=row(d),
        compiler_params=pltpu.CompilerParams(dimension_semantics=("parallel",), vmem_limit_bytes=VMEM_LIMIT),
        name="merge_xattn",
    )(x2, oa_t, ob, ga, gb, wa, wb, wo, g_x, wxq, kv, wxo)


def _ffn_kernel(x_ref, g_ref, wg_ref, wu_ref, wd_ref, gf_ref, y_ref, *, n_chunks):
    x = x_ref[...]
    hf = _rms(x, g_ref[...]).astype(BF16)
    d_ff = wg_ref.shape[1]
    cw = d_ff // n_chunks
    y = x
    for c in range(n_chunks):
        sl = slice(c * cw, (c + 1) * cw)
        g = _dot(hf, wg_ref[:, sl])
        u = _dot(hf, wu_ref[:, sl])
        act = (g * jax.nn.sigmoid(g) * u).astype(BF16)
        y = y + _dot(act, wd_ref[sl, :])
    y_ref[...] = _rms(y, gf_ref[...])


def _ffn(x2, g_ffn, wg, wu, wd, g_final, tm, n_chunks):
    t, d = x2.shape
    row = pl.BlockSpec((tm, d), lambda i: (i, 0))
    return pl.pallas_call(
        functools.partial(_ffn_kernel, n_chunks=n_chunks),
        out_shape=jax.ShapeDtypeStruct((t, d), F32),
        grid=(t // tm,),
        in_specs=[row, _full(g_ffn.shape), _full(wg.shape), _full(wu.shape), _full(wd.shape), _full(g_final.shape)],
        out_specs=row,
        compiler_params=pltpu.CompilerParams(dimension_semantics=("parallel",), vmem_limit_bytes=VMEM_LIMIT),
        name="ffn",
    )(x2, g_ffn, wg, wu, wd, g_final)


def kernel(x, mem, positions, g_mix, w_in, b_gate, g_q_lat, w_uq, g_kv_lat, w_ukv, w_a_proj, w_b_proj, w_o,
           g_x, g_mem, w_xq, w_xkv, w_xo, g_ffn, w_gate, w_up, w_down, g_final):
    b, s, d = x.shape
    m_len = mem.shape[1]
    assert g_mix.shape[0] == 1, "the final RMSNorm is fused into the (single) layer's FFN kernel"
    tm = min(1024, s)
    tm_in = min(1024, s)
    tq = min(1024, s)
    kb = min(256, s)
    rc = min(128, s)
    t = b * s

    inv_freq = ROPE_THETA ** (-jnp.arange(0, MLA_ROPE, 2, dtype=F32) / MLA_ROPE)
    invf = jnp.tile(inv_freq, 2 * _POS_GROUPS)[None, :]
    pos_c = positions.reshape(t // tm_in, _POS_GROUPS, tm_in // _POS_GROUPS).transpose(0, 2, 1)
    pos_c = jnp.repeat(pos_c.reshape(t // _POS_GROUPS, _POS_GROUPS), MLA_ROPE, axis=1)
    x2 = x.reshape(t, d)
    mem2 = mem.reshape(b * m_len, d)

    w_head, w_tail, wuq, wk, wv = _pack_in_weights(w_in[0], w_uq[0], w_ukv[0], d)
    q, k, vt, sbq, sbk, sbv, ga, gb = _in_proj(
        x2, pos_c, invf, g_mix[0][None, :], w_head, w_tail, b_gate[0], g_q_lat[0][None, :], wuq,
        g_kv_lat[0][None, :], wk, wv, tm_in, kb)
    o_a_t = _mla_attn(q.reshape(b, s, -1), k.reshape(b, s, -1), vt.reshape(b, s // kb, -1, kb), min(2048, s), kb,
                      kb, look=6, blocks_per_body=4)
    o_b = _sb_attn(sbq.reshape(b, s, -1), sbk.reshape(b, s, -1), sbv.reshape(b, s, -1), tq, kb, rc,
                   lag_a=3, lag_b=3)
    kv = _mem_kv(mem2, g_mem[0][None, :], w_xkv[0].astype(BF16))
    x2 = _merge(x2, o_a_t, o_b.reshape(t, -1), ga, gb,
                w_a_proj[0].astype(BF16), w_b_proj[0].astype(BF16), w_o[0].astype(BF16),
                g_x[0][None, :], w_xq[0].astype(BF16), kv, w_xo[0].astype(BF16), tm, s, m_len)
    x2 = _ffn(x2, g_ffn[0][None, :], w_gate[0].astype(BF16), w_up[0].astype(BF16), w_down[0].astype(BF16),
              g_final[None, :], tm, 1)
    return x2.reshape(b, s, d)
```

```python
import functools
import math

import jax
import jax.numpy as jnp
from jax import lax
from jax.experimental import pallas as pl
from jax.experimental.pallas import tpu as pltpu

F32 = jnp.float32
BF16 = jnp.bfloat16

EPS = 1e-6
ROPE_THETA = 10000.0
LOG2E = math.log2(math.e)

MLA_HEADS = 8
MLA_Q_RANK = 256
MLA_KV_RANK = 128
MLA_NOPE = 64
MLA_ROPE = 32
MLA_V = 64
SB_HEADS = 8
SB_HEAD_DIM = 64
X_HEADS = 4
X_HEAD_DIM = 128
LANES = 128
SUBLANES = 8
BF16_ROWS = 16
MLA_VT_ROWS = MLA_V + BF16_ROWS
NEG_BIG = -1e30
SB_UNDERFLOW_LOG2 = 160.0
SOFTPLUS_LINEAR_ABOVE = 64.0

VMEM_LIMIT = 56 * 1024 * 1024


def _rms(x, g):
    ms = jnp.mean(x * x, axis=-1, keepdims=True)
    return x * lax.rsqrt(ms + EPS) * g


def _dot(a, b):
    return jnp.dot(a, b, preferred_element_type=F32)


def _dot_nt(a, b):
    return lax.dot_general(a, b, (((1,), (1,)), ((), ())), preferred_element_type=F32)


def _pipelined(n_items, stages, lags):
    offs = [0]
    for lag in lags:
        offs.append(offs[-1] + lag)
    vals = [dict() for _ in stages]
    for tick in range(n_items + offs[-1]):
        for k, stage in enumerate(stages):
            i = tick - offs[k]
            if 0 <= i < n_items:
                prev = vals[k - 1].pop(i) if k > 0 else None
                vals[k][i] = stage(i, prev)


_C_CQ = 0
_C_CKV = _C_CQ + MLA_Q_RANK
_C_KRX = _C_CKV + MLA_KV_RANK
_SBW = SB_HEADS * SB_HEAD_DIM
_T_SBQ = 0
_T_SBK = _T_SBQ + _SBW
_T_SBV = _T_SBK + _SBW
_T_GA = _T_SBV + _SBW
_ROPE_END = MLA_NOPE + MLA_ROPE
_POS_GROUPS = LANES // MLA_ROPE


def _in_proj_kernel(x_ref, pos_ref, invf_ref, gmix_ref, whead_ref, wtail_ref, bg_ref, gq_ref, wuq_ref,
                    gkv_ref, wk_ref, wv_ref,
                    q_ref, k_ref, vt_ref, sbq_ref, sbk_ref, sbv_ref, ga_ref, gb_ref, *, d_model):
    x = x_ref[...]
    h = _rms(x, gmix_ref[...]).astype(BF16)

    def proj(c0, width, w_ref=wtail_ref):
        return _dot(h, w_ref[:, c0:c0 + width])

    cq_raw = proj(_C_CQ, MLA_Q_RANK, whead_ref)
    ckv_krx = proj(_C_CKV, MLA_KV_RANK + LANES, whead_ref)
    sbq_ref[...] = (proj(_T_SBQ, _SBW) * (LOG2E / math.sqrt(SB_HEAD_DIM))).astype(BF16)
    sbk_ref[...] = proj(_T_SBK, _SBW).astype(BF16)

    cq = _rms(cq_raw, gq_ref[...]).astype(BF16)
    ckv = _rms(ckv_krx[:, :MLA_KV_RANK], gkv_ref[...]).astype(BF16)
    qt = _dot(cq, wuq_ref[...])
    kn = _dot(ckv, wk_ref[...])
    vv = _dot(ckv, wv_ref[...])
    gate_a = proj(_T_GA, d_model)
    gate_b = proj(_T_GA + d_model, d_model)
    sbv_ref[...] = proj(_T_SBV, _SBW).astype(BF16)

    lane = lax.broadcasted_iota(jnp.int32, (1, LANES), 1)
    ang = pos_ref[...].astype(F32) * invf_ref[...]
    cos_c = jnp.cos(ang)
    sin_c = jnp.sin(ang)
    is_rope = (lane >= MLA_NOPE) & (lane < _ROPE_END)
    cos_parts, sin_parts = [], []
    for g in range(_POS_GROUPS):
        shift = (MLA_NOPE - g * MLA_ROPE) % LANES
        cos_g = cos_c if shift == 0 else pltpu.roll(cos_c, shift, axis=1)
        sin_g = sin_c if shift == 0 else pltpu.roll(sin_c, shift, axis=1)
        cos_parts.append(jnp.where(is_rope, cos_g, jnp.where(lane < MLA_NOPE, 1.0, 0.0)))
        sin_parts.append(jnp.where(is_rope, sin_g, 0.0))
    cos = jnp.concatenate(cos_parts, axis=0)
    sin = jnp.concatenate(sin_parts, axis=0)
    partner_shift = LANES - MLA_ROPE

    def rotary(t, c, s):
        return t * c + pltpu.roll(t, partner_shift, axis=1) * s

    q_scale = LOG2E / math.sqrt(MLA_NOPE + MLA_ROPE)
    cos_q = cos * q_scale
    sin_q = sin * q_scale
    for hd in range(MLA_HEADS):
        sl = slice(hd * LANES, (hd + 1) * LANES)
        q_ref[:, sl] = rotary(qt[:, sl], cos_q, sin_q).astype(BF16)

    krope = rotary(ckv_krx[:, MLA_KV_RANK:], cos, sin)
    for hd in range(MLA_HEADS):
        sl = slice(hd * LANES, (hd + 1) * LANES)
        k_ref[:, sl] = (kn[:, sl] + krope).astype(BF16)

    v_lane = lax.broadcasted_iota(jnp.int32, (1, MLA_HEADS * LANES), 1)
    ones_cols = jnp.where(v_lane % LANES >= MLA_V, 1.0, 0.0).astype(F32)
    vv = vv + ones_cols
    n_kblk, _, kb = vt_ref.shape
    for c in range(n_kblk):
        for hd in range(MLA_HEADS):
            v_t = vv[c * kb:(c + 1) * kb, hd * LANES:(hd + 1) * LANES].T
            vt_ref[c, hd * MLA_VT_ROWS:(hd + 1) * MLA_VT_ROWS, :] = v_t[:MLA_VT_ROWS].astype(BF16)

    ga_ref[...] = jax.nn.sigmoid(gate_a + bg_ref[0:1, :]).astype(BF16)
    gb_ref[...] = jax.nn.sigmoid(gate_b + bg_ref[1:2, :]).astype(BF16)


def _pack_in_weights(w_in, w_uq, w_ukv, d_model):
    f = w_in.dtype
    zeros = lambda n: jnp.zeros((d_model, n), f)
    o_kr = MLA_Q_RANK + MLA_KV_RANK
    o_sb = o_kr + MLA_ROPE
    half = MLA_ROPE // 2
    assert LANES - MLA_NOPE - MLA_ROPE == MLA_ROPE
    kr = w_in[:, o_kr:o_kr + MLA_ROPE]
    kr_rot = jnp.concatenate([-kr[:, half:], kr[:, :half]], axis=1)
    w_head = jnp.concatenate([w_in[:, :o_kr], zeros(MLA_NOPE), kr, kr_rot], axis=1).astype(BF16)
    w_tail = w_in[:, o_sb:].astype(BF16)

    r = w_uq.shape[0]
    wq = w_uq.reshape(r, MLA_HEADS, MLA_NOPE + MLA_ROPE)
    rope = wq[:, :, MLA_NOPE:]
    rope_rot = jnp.concatenate([-rope[:, :, half:], rope[:, :, :half]], axis=2)
    wuq = jnp.concatenate([wq, rope_rot], axis=2).reshape(r, MLA_HEADS * LANES).astype(BF16)

    rk = w_ukv.shape[0]
    wkv = w_ukv.reshape(rk, MLA_HEADS, MLA_NOPE + MLA_V)
    wk = jnp.concatenate([wkv[:, :, :MLA_NOPE], jnp.zeros((rk, MLA_HEADS, LANES - MLA_NOPE), f)], axis=2)
    wk = wk.reshape(rk, MLA_HEADS * LANES).astype(BF16)
    wv = jnp.concatenate([wkv[:, :, MLA_NOPE:], jnp.zeros((rk, MLA_HEADS, LANES - MLA_V), f)], axis=2)
    wv = wv.reshape(rk, MLA_HEADS * LANES).astype(BF16)
    return w_head, w_tail, wuq, wk, wv


def _full(shape):
    n = len(shape)
    return pl.BlockSpec(shape, lambda *_: (0,) * n, pipeline_mode=pl.Buffered(1))


def _in_proj(x2, pos_c, invf, g_mix, w_head, w_tail, b_gate, g_q, wuq, g_kv, wk, wv, tm, kb):
    t, d = x2.shape
    row = lambda w: pl.BlockSpec((tm, w), lambda i: (i, 0))
    pos_spec = pl.BlockSpec((tm // _POS_GROUPS, LANES), lambda i: (i, 0))
    hw = MLA_HEADS * LANES
    out_widths = [hw, hw, None, _SBW, _SBW, _SBW, d, d]
    vt_shape = jax.ShapeDtypeStruct((t // kb, MLA_HEADS * MLA_VT_ROWS, kb), BF16)
    vt_spec = pl.BlockSpec((tm // kb, MLA_HEADS * MLA_VT_ROWS, kb), lambda i: (i, 0, 0))
    return pl.pallas_call(
        functools.partial(_in_proj_kernel, d_model=d),
        out_shape=[vt_shape if w is None else jax.ShapeDtypeStruct((t, w), BF16) for w in out_widths],
        grid=(t // tm,),
        in_specs=[row(d), pos_spec, _full(invf.shape), _full(g_mix.shape), _full(w_head.shape), _full(w_tail.shape),
                  _full(b_gate.shape), _full(g_q.shape), _full(wuq.shape),
                  _full(g_kv.shape), _full(wk.shape), _full(wv.shape)],
        out_specs=[vt_spec if w is None else row(w) for w in out_widths],
        compiler_params=pltpu.CompilerParams(dimension_semantics=("parallel",), vmem_limit_bytes=VMEM_LIMIT),
        name="in_proj",
    )(x2, pos_c, invf, g_mix, w_head, w_tail, b_gate, g_q, wuq, g_kv, wk, wv)


def _work_items(blocks, n_chunks, rc, kb, strict):
    items = []
    for boff, diag in blocks:
        for j in range(2):
            for r in range(n_chunks):
                if diag is None:
                    items.append((boff, diag, j, r, False))
                    continue
                lo_row, hi_row = r * rc, (r + 1) * rc - 1
                lo_col, hi_col = diag * kb, (diag + 1) * kb - 1
                if (lo_col >= hi_row) if strict else (lo_col > hi_row):
                    continue
                fully_visible = (hi_col < lo_row) if strict else (hi_col <= lo_row)
                items.append((boff, diag, j, r, not fully_visible))
    return items


def _tile_rows(x, n):
    return jnp.concatenate([x] * n, axis=0)


def _mla_kernel(q_ref, k_ref, vt_ref, o_ref, m_sc, acc_sc, *, tq, kb, rc, look, blocks_per_body):
    qi = pl.program_id(2)
    nd = tq // kb
    nrc = tq // rc
    key_i = lax.broadcasted_iota(jnp.int32, (kb, rc), 0)
    qry_i = lax.broadcasted_iota(jnp.int32, (kb, rc), 1)
    m_sc[...] = jnp.full(m_sc.shape, NEG_BIG, F32)
    acc_sc[...] = jnp.zeros(acc_sc.shape, F32)

    def region(base_blk, work):
        def scores(i, _):
            boff, _diag, j, r, _mk = work[i]
            hs = slice(j * LANES, (j + 1) * LANES)
            keys = pl.ds(pl.multiple_of((base_blk + boff) * kb, kb), kb)
            return _dot_nt(k_ref[keys, hs], q_ref[r * rc:(r + 1) * rc, hs])

        def finish(i, s):
            boff, diag, j, r, mk = work[i]
            qs = slice(r * rc, (r + 1) * rc)
            if mk:
                s = jnp.where(key_i + (diag * kb - r * rc) <= qry_i, s, NEG_BIG)
            m_prev = m_sc[j, :, qs]
            m_new = jnp.maximum(m_prev, jnp.max(s, axis=0, keepdims=True))
            alpha = jnp.exp2(m_prev - m_new)
            p = jnp.exp2(s - _tile_rows(m_new, kb // SUBLANES))
            v_t = vt_ref[base_blk + boff, j * MLA_VT_ROWS:(j + 1) * MLA_VT_ROWS, :]
            pv = _dot(v_t, p.astype(BF16))
            acc_sc[j, :, qs] = _tile_rows(alpha, MLA_VT_ROWS // SUBLANES) * acc_sc[j, :, qs] + pv
            m_sc[j, :, qs] = m_new

        _pipelined(len(work), [scores, finish], [look])

    def full_work(n_blocks):
        return _work_items([(o, None) for o in range(n_blocks)], nrc, rc, kb, strict=False)

    n_full = qi * nd
    n_big = n_full // blocks_per_body

    def body(i, carry):
        region(i * blocks_per_body, full_work(blocks_per_body))
        return carry

    lax.fori_loop(0, n_big, body, 0)
    rem = n_full - n_big * blocks_per_body
    for tiles in range(1, blocks_per_body // nd):
        @pl.when(rem == tiles * nd)
        def _():
            region(n_big * blocks_per_body, full_work(tiles * nd))

    region(n_full, _work_items([(d, d) for d in range(nd)], nrc, rc, kb, strict=False))

    for j in range(2):
        denom = _tile_rows(acc_sc[j, MLA_V:MLA_V + SUBLANES, :], MLA_V // SUBLANES)
        o_ref[j * MLA_V:(j + 1) * MLA_V, :] = (acc_sc[j, :MLA_V, :] / denom).astype(o_ref.dtype)


def _mla_attn(q, k, vt, tq, kb, rc, look, blocks_per_body):
    b, s, _ = q.shape
    hp = MLA_HEADS // 2
    assert blocks_per_body % (tq // kb) == 0 or (tq // kb) % blocks_per_body == 0
    return pl.pallas_call(
        functools.partial(_mla_kernel, tq=tq, kb=kb, rc=rc, look=look, blocks_per_body=blocks_per_body),
        out_shape=jax.ShapeDtypeStruct((b, MLA_HEADS * MLA_V, s), BF16),
        grid=(b, hp, s // tq),
        in_specs=[pl.BlockSpec((None, tq, 2 * LANES), lambda bi, h, i: (bi, i, h)),
                  pl.BlockSpec((None, s, 2 * LANES), lambda bi, h, i: (bi, 0, h)),
                  pl.BlockSpec((None, s // kb, 2 * MLA_VT_ROWS, kb), lambda bi, h, i: (bi, 0, h, 0))],
        out_specs=pl.BlockSpec((None, 2 * MLA_V, tq), lambda bi, h, i: (bi, h, i)),
        scratch_shapes=[pltpu.VMEM((2, SUBLANES, tq), F32), pltpu.VMEM((2, MLA_VT_ROWS, tq), F32)],
        compiler_params=pltpu.CompilerParams(dimension_semantics=("parallel", "parallel", "parallel"),
                                             vmem_limit_bytes=VMEM_LIMIT),
        name="mla_attn",
    )(q, k, vt)


def _sb_kernel(q_ref, k_ref, v_ref, tri_ref, o_ref, qz_sc, r_sc, acc_sc, *, tq, kb, rc, lag_a, lag_b):
    qi = pl.program_id(2)
    nd = tq // kb
    nrc = tq // rc
    rows = lax.broadcasted_iota(jnp.int32, (rc, kb), 0)
    cols = lax.broadcasted_iota(jnp.int32, (rc, kb), 1)
    lane = lax.broadcasted_iota(jnp.int32, (tq, LANES), 1)
    q_both = q_ref[...]
    for j in range(2):
        in_head = (lane >= j * SB_HEAD_DIM) & (lane < (j + 1) * SB_HEAD_DIM)
        qz_sc[j] = jnp.where(in_head, q_both, jnp.zeros_like(q_both))
    r_sc[...] = jnp.zeros(r_sc.shape, F32)
    acc_sc[...] = jnp.zeros(acc_sc.shape, F32)

    def region(base_blk, work, tracked=(None,)):
        r_min = [None] * len(tracked)

        def kv_rows(i):
            return pl.ds(pl.multiple_of((base_blk + work[i][0]) * kb, kb), kb)

        def mask_of(i):
            _boff, diag, _j, r, _mk = work[i]
            return cols + (diag * kb - r * rc) < rows

        def scores(i, _):
            _boff, _diag, j, r, _mk = work[i]
            return _dot_nt(qz_sc[j, r * rc:(r + 1) * rc, :], k_ref[kv_rows(i), :])

        def suffix(i, z):
            _boff, _diag, j, r, mk = work[i]
            rs = slice(r * rc, (r + 1) * rc)
            sp = jnp.where(z > SOFTPLUS_LINEAR_ABOVE, z, jnp.log(1.0 + jnp.exp2(z)) * LOG2E)
            if mk:
                sp = jnp.where(mask_of(i), sp, 0.0)
            c = _dot(sp.astype(BF16), tri_ref[...])
            r_prev = r_sc[j, rs, :]
            r_new = r_prev + jnp.sum(sp, axis=-1, keepdims=True)
            r_sc[j, rs, :] = r_new
            wanted = [n for n, members in enumerate(tracked) if members is None or i in members]
            if wanted:
                parts = [r_new[t8 * SUBLANES:(t8 + 1) * SUBLANES, :] for t8 in range(rc // SUBLANES)]
                while len(parts) > 1:
                    parts = [jnp.minimum(a, b) for a, b in zip(parts[0::2], parts[1::2])]
                for n in wanted:
                    r_min[n] = parts[0] if r_min[n] is None else jnp.minimum(r_min[n], parts[0])
            return z, c, r_prev

        def weights(i, zcr):
            _boff, _diag, j, r, mk = work[i]
            rs = slice(r * rc, (r + 1) * rc)
            z, c, r_prev = zcr
            a = jnp.exp2(z - c - jnp.concatenate([r_prev] * (kb // LANES), axis=1))
            if mk:
                a = jnp.where(mask_of(i), a, 0.0)
            acc_sc[j, rs, :] += _dot(a.astype(BF16), v_ref[kv_rows(i), :])

        _pipelined(len(work), [scores, suffix, weights], [lag_a, lag_b])
        return r_min

    def still_open(low):
        return (jnp.min(low) <= SB_UNDERFLOW_LOG2).astype(jnp.int32)

    diag_items = _work_items([(d, d) for d in reversed(range(nd))], nrc, rc, kb, strict=True)
    own_block = lambda r: ((r + 1) * rc - 1) // kb
    near_items = [it for it in diag_items if it[1] >= own_block(it[3]) - 1]
    far_items = [it for it in diag_items if it[1] < own_block(it[3]) - 1]
    far_chunks = {(it[2], it[3]) for it in far_items}
    n_top = kb // rc
    last_near = [i for i, it in enumerate(near_items) if it[1] == max(own_block(it[3]) - 1, 0)]
    far_deciding = {i for i in last_near if (near_items[i][2], near_items[i][3]) in far_chunks}
    lower_deciding = {i for i in last_near if near_items[i][3] >= n_top}
    far_low, lower_low = region(qi * nd, near_items, tracked=(far_deciding, lower_deciding))
    if far_items:
        @pl.when(still_open(far_low) > 0)
        def _():
            region(qi * nd, far_items, tracked=())

    n_full = qi * nd
    full_work = _work_items([(0, None)], nrc, rc, kb, strict=True)
    top_work = [it for it in full_work if it[3] < n_top]
    lower_work = [it for it in full_work if it[3] >= n_top]

    def cond(carry):
        i, top_open, lower_open = carry
        return jnp.logical_and(i < n_full, (top_open + lower_open) > 0)

    def body(carry):
        i, _, lower_open = carry
        blk = n_full - 1 - i
        top_open = still_open(region(blk, top_work)[0])
        lower_open = lax.cond(lower_open > 0, lambda: still_open(region(blk, lower_work)[0]),
                              lambda: jnp.int32(0))
        return i + 1, top_open, lower_open

    lower_open0 = still_open(lower_low) if lower_work else jnp.int32(0)
    lax.while_loop(cond, body, (jnp.int32(0), jnp.int32(1), lower_open0))
    o_ref[...] = jnp.where(lane < SB_HEAD_DIM, acc_sc[0], acc_sc[1]).astype(o_ref.dtype)


def _sb_attn(q, k, v, tq, kb, rc, lag_a, lag_b):
    b, s, _ = q.shape
    hp = SB_HEADS // 2
    idx = jnp.arange(kb)
    tri = (idx[:, None] >= idx[None, :]).astype(BF16)
    return pl.pallas_call(
        functools.partial(_sb_kernel, tq=tq, kb=kb, rc=rc, lag_a=lag_a, lag_b=lag_b),
        out_shape=jax.ShapeDtypeStruct((b, s, _SBW), BF16),
        grid=(b, hp, s // tq),
        in_specs=[pl.BlockSpec((None, tq, LANES), lambda bi, h, i: (bi, i, h)),
                  pl.BlockSpec((None, s, LANES), lambda bi, h, i: (bi, 0, h)),
                  pl.BlockSpec((None, s, LANES), lambda bi, h, i: (bi, 0, h)),
                  pl.BlockSpec((kb, kb), lambda bi, h, i: (0, 0))],
        out_specs=pl.BlockSpec((None, tq, LANES), lambda bi, h, i: (bi, i, h)),
        scratch_shapes=[pltpu.VMEM((2, tq, LANES), BF16), pltpu.VMEM((2, tq, LANES), F32),
                        pltpu.VMEM((2, tq, LANES), F32)],
        compiler_params=pltpu.CompilerParams(dimension_semantics=("parallel", "parallel", "parallel"),
                                             vmem_limit_bytes=VMEM_LIMIT),
        name="sb_attn",
    )(q, k, v, tri)


def _mem_kv_kernel(mem_ref, g_ref, w_ref, kv_ref):
    mn = _rms(mem_ref[...], g_ref[...]).astype(BF16)
    kv_ref[...] = _dot(mn, w_ref[...]).astype(BF16)


def _mem_kv(mem2, g_mem, w_xkv):
    rows, _ = mem2.shape
    n = w_xkv.shape[1]
    return pl.pallas_call(
        _mem_kv_kernel,
        out_shape=jax.ShapeDtypeStruct((rows, n), BF16),
        grid=(1,),
        in_specs=[_full(mem2.shape), _full(g_mem.shape), _full(w_xkv.shape)],
        out_specs=_full((rows, n)),
        compiler_params=pltpu.CompilerParams(vmem_limit_bytes=VMEM_LIMIT),
        name="mem_kv",
    )(mem2, g_mem, w_xkv)


def _merge_kernel(x_ref, oa_ref, ob_ref, ga_ref, gb_ref, wa_ref, wb_ref, wo_ref, gx_ref, wxq_ref, kv_ref,
                  wxo_ref, y_ref):
    pa = lax.dot_general(oa_ref[...], wa_ref[...], (((0,), (0,)), ((), ())), preferred_element_type=F32)
    merged = ga_ref[...].astype(F32) * pa + gb_ref[...].astype(F32) * _dot(ob_ref[...], wb_ref[...])
    x1 = x_ref[...] + _dot(merged.astype(BF16), wo_ref[...])

    hx = _rms(x1, gx_ref[...]).astype(BF16)
    xw = X_HEADS * X_HEAD_DIM
    xq = (_dot(hx, wxq_ref[...]) * (LOG2E / math.sqrt(X_HEAD_DIM))).astype(BF16)
    heads = []
    for hd in range(X_HEADS):
        sl = slice(hd * X_HEAD_DIM, (hd + 1) * X_HEAD_DIM)
        kh = kv_ref[:, sl]
        vh = kv_ref[:, xw + hd * X_HEAD_DIM: xw + (hd + 1) * X_HEAD_DIM]
        s = _dot_nt(xq[:, sl], kh)
        p = jnp.exp2(s - jnp.max(s, axis=-1, keepdims=True))
        l = jnp.sum(p, axis=-1, keepdims=True)
        heads.append((_dot(p.astype(BF16), vh) / l).astype(BF16))
    xo = jnp.concatenate(heads, axis=1)
    y_ref[...] = x1 + _dot(xo, wxo_ref[...])


def _merge(x2, oa_t, ob, ga, gb, wa, wb, wo, g_x, wxq, kv, wxo, tm, seq, m_len):
    t, d = x2.shape
    per_b = seq // tm
    row = lambda w: pl.BlockSpec((tm, w), lambda i: (i, 0))
    return pl.pallas_call(
        _merge_kernel,
        out_shape=jax.ShapeDtypeStruct((t, d), F32),
        grid=(t // tm,),
        in_specs=[row(d), pl.BlockSpec((None, oa_t.shape[1], tm), lambda i: (i // per_b, 0, i % per_b)),
                  row(ob.shape[1]), row(d), row(d),
                  _full(wa.shape), _full(wb.shape), _full(wo.shape), _full(g_x.shape), _full(wxq.shape),
                  pl.BlockSpec((m_len, kv.shape[1]), lambda i: (i // per_b, 0)),
                  _full(wxo.shape)],
        out_specs=row(d),
        compiler_params=pltpu.CompilerParams(dimension_semantics=("parallel",), vmem_limit_bytes=VMEM_LIMIT),
        name="merge_xattn",
    )(x2, oa_t, ob, ga, gb, wa, wb, wo, g_x, wxq, kv, wxo)


def _ffn_kernel(x_ref, g_ref, wg_ref, wu_ref, wd_ref, gf_ref, y_ref, *, n_chunks):
    x = x_ref[...]
    hf = _rms(x, g_ref[...]).astype(BF16)
    d_ff = wg_ref.shape[1]
    cw = d_ff // n_chunks
    y = x
    for c in range(n_chunks):
        sl = slice(c * cw, (c + 1) * cw)
        g = _dot(hf, wg_ref[:, sl])
        u = _dot(hf, wu_ref[:, sl])
        act = (g * jax.nn.sigmoid(g) * u).astype(BF16)
        y = y + _dot(act, wd_ref[sl, :])
    y_ref[...] = _rms(y, gf_ref[...])


def _ffn(x2, g_ffn, wg, wu, wd, g_final, tm, n_chunks):
    t, d = x2.shape
    row = pl.BlockSpec((tm, d), lambda i: (i, 0))
    return pl.pallas_call(
        functools.partial(_ffn_kernel, n_chunks=n_chunks),
        out_shape=jax.ShapeDtypeStruct((t, d), F32),
        grid=(t // tm,),
        in_specs=[row, _full(g_ffn.shape), _full(wg.shape), _full(wu.shape), _full(wd.shape), _full(g_final.shape)],
        out_specs=row,
        compiler_params=pltpu.CompilerParams(dimension_semantics=("parallel",), vmem_limit_bytes=VMEM_LIMIT),
        name="ffn",
    )(x2, g_ffn, wg, wu, wd, g_final)


def kernel(x, mem, positions, g_mix, w_in, b_gate, g_q_lat, w_uq, g_kv_lat, w_ukv, w_a_proj, w_b_proj, w_o,
           g_x, g_mem, w_xq, w_xkv, w_xo, g_ffn, w_gate, w_up, w_down, g_final):
    b, s, d = x.shape
    m_len = mem.shape[1]
    assert g_mix.shape[0] == 1, "the final RMSNorm is fused into the (single) layer's FFN kernel"
    tm = min(1024, s)
    tm_in = min(1024, s)
    tq = min(1024, s)
    kb = min(256, s)
    rc = min(128, s)
    t = b * s

    inv_freq = ROPE_THETA ** (-jnp.arange(0, MLA_ROPE, 2, dtype=F32) / MLA_ROPE)
    invf = jnp.tile(inv_freq, 2 * _POS_GROUPS)[None, :]
    pos_c = positions.reshape(t // tm_in, _POS_GROUPS, tm_in // _POS_GROUPS).transpose(0, 2, 1)
    pos_c = jnp.repeat(pos_c.reshape(t // _POS_GROUPS, _POS_GROUPS), MLA_ROPE, axis=1)
    x2 = x.reshape(t, d)
    mem2 = mem.reshape(b * m_len, d)

    w_head, w_tail, wuq, wk, wv = _pack_in_weights(w_in[0], w_uq[0], w_ukv[0], d)
    q, k, vt, sbq, sbk, sbv, ga, gb = _in_proj(
        x2, pos_c, invf, g_mix[0][None, :], w_head, w_tail, b_gate[0], g_q_lat[0][None, :], wuq,
        g_kv_lat[0][None, :], wk, wv, tm_in, kb)
    o_a_t = _mla_attn(q.reshape(b, s, -1), k.reshape(b, s, -1), vt.reshape(b, s // kb, -1, kb), min(2048, s), kb,
                      kb, look=8, blocks_per_body=8)
    o_b = _sb_attn(sbq.reshape(b, s, -1), sbk.reshape(b, s, -1), sbv.reshape(b, s, -1), tq, kb, rc,
                   lag_a=3, lag_b=3)
    kv = _mem_kv(mem2, g_mem[0][None, :], w_xkv[0].astype(BF16))
    x2 = _merge(x2, o_a_t, o_b.reshape(t, -1), ga, gb,
                w_a_proj[0].astype(BF16), w_b_proj[0].astype(BF16), w_o[0].astype(BF16),
                g_x[0][None, :], w_xq[0].astype(BF16), kv, w_xo[0].astype(BF16), tm, s, m_len)
    x2 = _ffn(x2, g_ffn[0][None, :], w_gate[0].astype(BF16), w_up[0].astype(BF16), w_down[0].astype(BF16),
              g_final[None, :], tm, 1)
    return x2.reshape(b, s, d)
```

```python
import functools
import math

import jax
import jax.numpy as jnp
from jax import lax
from jax.experimental import pallas as pl
from jax.experimental.pallas import tpu as pltpu

F32 = jnp.float32
BF16 = jnp.bfloat16

EPS = 1e-6
ROPE_THETA = 10000.0
LOG2E = math.log2(math.e)

MLA_HEADS = 8
MLA_Q_RANK = 256
MLA_KV_RANK = 128
MLA_NOPE = 64
MLA_ROPE = 32
MLA_V = 64
SB_HEADS = 8
SB_HEAD_DIM = 64
X_HEADS = 4
X_HEAD_DIM = 128
LANES = 128
SUBLANES = 8
BF16_ROWS = 16
MLA_VT_ROWS = MLA_V + BF16_ROWS
NEG_BIG = -1e30
SB_UNDERFLOW_LOG2 = 160.0
SOFTPLUS_LINEAR_ABOVE = 64.0

VMEM_LIMIT = 56 * 1024 * 1024


def _rms(x, g):
    ms = jnp.mean(x * x, axis=-1, keepdims=True)
    return x * lax.rsqrt(ms + EPS) * g


def _dot(a, b):
    return jnp.dot(a, b, preferred_element_type=F32)


def _dot_nt(a, b):
    return lax.dot_general(a, b, (((1,), (1,)), ((), ())), preferred_element_type=F32)


def _pipelined(n_items, stages, lags):
    offs = [0]
    for lag in lags:
        offs.append(offs[-1] + lag)
    vals = [dict() for _ in stages]
    for tick in range(n_items + offs[-1]):
        for k, stage in enumerate(stages):
            i = tick - offs[k]
            if 0 <= i < n_items:
                prev = vals[k - 1].pop(i) if k > 0 else None
                vals[k][i] = stage(i, prev)


_C_CQ = 0
_C_CKV = _C_CQ + MLA_Q_RANK
_C_KRX = _C_CKV + MLA_KV_RANK
_SBW = SB_HEADS * SB_HEAD_DIM
_T_SBQ = 0
_T_SBK = _T_SBQ + _SBW
_T_SBV = _T_SBK + _SBW
_T_GA = _T_SBV + _SBW
_ROPE_END = MLA_NOPE + MLA_ROPE
_POS_GROUPS = LANES // MLA_ROPE


def _in_proj_kernel(x_ref, pos_ref, invf_ref, gmix_ref, whead_ref, wtail_ref, bg_ref, gq_ref, wuq_ref,
                    gkv_ref, wk_ref, wv_ref,
                    q_ref, k_ref, vt_ref, sbq_ref, sbk_ref, sbv_ref, ga_ref, gb_ref, *, d_model):
    x = x_ref[...]
    h = _rms(x, gmix_ref[...]).astype(BF16)

    def proj(c0, width, w_ref=wtail_ref):
        return _dot(h, w_ref[:, c0:c0 + width])

    cq_raw = proj(_C_CQ, MLA_Q_RANK, whead_ref)
    ckv_krx = proj(_C_CKV, MLA_KV_RANK + LANES, whead_ref)
    sbq_ref[...] = (proj(_T_SBQ, _SBW) * (LOG2E / math.sqrt(SB_HEAD_DIM))).astype(BF16)
    sbk_ref[...] = proj(_T_SBK, _SBW).astype(BF16)

    cq = _rms(cq_raw, gq_ref[...]).astype(BF16)
    ckv = _rms(ckv_krx[:, :MLA_KV_RANK], gkv_ref[...]).astype(BF16)
    qt = _dot(cq, wuq_ref[...])
    kn = _dot(ckv, wk_ref[...])
    vv = _dot(ckv, wv_ref[...])
    gate_a = proj(_T_GA, d_model)
    gate_b = proj(_T_GA + d_model, d_model)
    sbv_ref[...] = proj(_T_SBV, _SBW).astype(BF16)

    lane = lax.broadcasted_iota(jnp.int32, (1, LANES), 1)
    ang = pos_ref[...].astype(F32) * invf_ref[...]
    cos_c = jnp.cos(ang)
    sin_c = jnp.sin(ang)
    is_rope = (lane >= MLA_NOPE) & (lane < _ROPE_END)
    cos_parts, sin_parts = [], []
    for g in range(_POS_GROUPS):
        shift = (MLA_NOPE - g * MLA_ROPE) % LANES
        cos_g = cos_c if shift == 0 else pltpu.roll(cos_c, shift, axis=1)
        sin_g = sin_c if shift == 0 else pltpu.roll(sin_c, shift, axis=1)
        cos_parts.append(jnp.where(is_rope, cos_g, jnp.where(lane < MLA_NOPE, 1.0, 0.0)))
        sin_parts.append(jnp.where(is_rope, sin_g, 0.0))
    cos = jnp.concatenate(cos_parts, axis=0)
    sin = jnp.concatenate(sin_parts, axis=0)
    partner_shift = LANES - MLA_ROPE

    def rotary(t, c, s):
        return t * c + pltpu.roll(t, partner_shift, axis=1) * s

    q_scale = LOG2E / math.sqrt(MLA_NOPE + MLA_ROPE)
    cos_q = cos * q_scale
    sin_q = sin * q_scale
    for hd in range(MLA_HEADS):
        sl = slice(hd * LANES, (hd + 1) * LANES)
        q_ref[:, sl] = rotary(qt[:, sl], cos_q, sin_q).astype(BF16)

    krope = rotary(ckv_krx[:, MLA_KV_RANK:], cos, sin)
    for hd in range(MLA_HEADS):
        sl = slice(hd * LANES, (hd + 1) * LANES)
        k_ref[:, sl] = (kn[:, sl] + krope).astype(BF16)

    v_lane = lax.broadcasted_iota(jnp.int32, (1, MLA_HEADS * LANES), 1)
    ones_cols = jnp.where(v_lane % LANES >= MLA_V, 1.0, 0.0).astype(F32)
    vv = vv + ones_cols
    n_kblk, _, kb = vt_ref.shape
    for c in range(n_kblk):
        for hd in range(MLA_HEADS):
            v_t = vv[c * kb:(c + 1) * kb, hd * LANES:(hd + 1) * LANES].T
            vt_ref[c, hd * MLA_VT_ROWS:(hd + 1) * MLA_VT_ROWS, :] = v_t[:MLA_VT_ROWS].astype(BF16)

    ga_ref[...] = jax.nn.sigmoid(gate_a + bg_ref[0:1, :]).astype(BF16)
    gb_ref[...] = jax.nn.sigmoid(gate_b + bg_ref[1:2, :]).astype(BF16)


def _pack_in_weights(w_in, w_uq, w_ukv, d_model):
    f = w_in.dtype
    zeros = lambda n: jnp.zeros((d_model, n), f)
    o_kr = MLA_Q_RANK + MLA_KV_RANK
    o_sb = o_kr + MLA_ROPE
    half = MLA_ROPE // 2
    assert LANES - MLA_NOPE - MLA_ROPE == MLA_ROPE
    kr = w_in[:, o_kr:o_kr + MLA_ROPE]
    kr_rot = jnp.concatenate([-kr[:, half:], kr[:, :half]], axis=1)
    w_head = jnp.concatenate([w_in[:, :o_kr], zeros(MLA_NOPE), kr, kr_rot], axis=1).astype(BF16)
    w_tail = w_in[:, o_sb:].astype(BF16)

    r = w_uq.shape[0]
    wq = w_uq.reshape(r, MLA_HEADS, MLA_NOPE + MLA_ROPE)
    rope = wq[:, :, MLA_NOPE:]
    rope_rot = jnp.concatenate([-rope[:, :, half:], rope[:, :, :half]], axis=2)
    wuq = jnp.concatenate([wq, rope_rot], axis=2).reshape(r, MLA_HEADS * LANES).astype(BF16)

    rk = w_ukv.shape[0]
    wkv = w_ukv.reshape(rk, MLA_HEADS, MLA_NOPE + MLA_V)
    wk = jnp.concatenate([wkv[:, :, :MLA_NOPE], jnp.zeros((rk, MLA_HEADS, LANES - MLA_NOPE), f)], axis=2)
    wk = wk.reshape(rk, MLA_HEADS * LANES).astype(BF16)
    wv = jnp.concatenate([wkv[:, :, MLA_NOPE:], jnp.zeros((rk, MLA_HEADS, LANES - MLA_V), f)], axis=2)
    wv = wv.reshape(rk, MLA_HEADS * LANES).astype(BF16)
    return w_head, w_tail, wuq, wk, wv


def _full(shape):
    n = len(shape)
    return pl.BlockSpec(shape, lambda *_: (0,) * n, pipeline_mode=pl.Buffered(1))


def _in_proj(x2, pos_c, invf, g_mix, w_head, w_tail, b_gate, g_q, wuq, g_kv, wk, wv, tm, kb):
    t, d = x2.shape
    row = lambda w: pl.BlockSpec((tm, w), lambda i: (i, 0))
    pos_spec = pl.BlockSpec((tm // _POS_GROUPS, LANES), lambda i: (i, 0))
    hw = MLA_HEADS * LANES
    out_widths = [hw, hw, None, _SBW, _SBW, _SBW, d, d]
    vt_shape = jax.ShapeDtypeStruct((t // kb, MLA_HEADS * MLA_VT_ROWS, kb), BF16)
    vt_spec = pl.BlockSpec((tm // kb, MLA_HEADS * MLA_VT_ROWS, kb), lambda i: (i, 0, 0))
    return pl.pallas_call(
        functools.partial(_in_proj_kernel, d_model=d),
        out_shape=[vt_shape if w is None else jax.ShapeDtypeStruct((t, w), BF16) for w in out_widths],
        grid=(t // tm,),
        in_specs=[row(d), pos_spec, _full(invf.shape), _full(g_mix.shape), _full(w_head.shape), _full(w_tail.shape),
                  _full(b_gate.shape), _full(g_q.shape), _full(wuq.shape),
                  _full(g_kv.shape), _full(wk.shape), _full(wv.shape)],
        out_specs=[vt_spec if w is None else row(w) for w in out_widths],
        compiler_params=pltpu.CompilerParams(dimension_semantics=("parallel",), vmem_limit_bytes=VMEM_LIMIT),
        name="in_proj",
    )(x2, pos_c, invf, g_mix, w_head, w_tail, b_gate, g_q, wuq, g_kv, wk, wv)


def _work_items(blocks, n_chunks, rc, kb, strict):
    items = []
    for boff, diag in blocks:
        for j in range(2):
            for r in range(n_chunks):
                if diag is None:
                    items.append((boff, diag, j, r, False))
                    continue
                lo_row, hi_row = r * rc, (r + 1) * rc - 1
                lo_col, hi_col = diag * kb, (diag + 1) * kb - 1
                if (lo_col >= hi_row) if strict else (lo_col > hi_row):
                    continue
                fully_visible = (hi_col < lo_row) if strict else (hi_col <= lo_row)
                items.append((boff, diag, j, r, not fully_visible))
    return items


def _tile_rows(x, n):
    return jnp.concatenate([x] * n, axis=0)


def _mla_kernel(q_ref, k_ref, vt_ref, o_ref, m_sc, acc_sc, *, tq, kb, rc, look, blocks_per_body):
    qi = pl.program_id(2)
    nd = tq // kb
    nrc = tq // rc
    key_i = lax.broadcasted_iota(jnp.int32, (kb, rc), 0)
    qry_i = lax.broadcasted_iota(jnp.int32, (kb, rc), 1)
    m_sc[...] = jnp.full(m_sc.shape, NEG_BIG, F32)
    acc_sc[...] = jnp.zeros(acc_sc.shape, F32)

    def region(base_blk, work):
        def scores(i, _):
            boff, _diag, j, r, _mk = work[i]
            hs = slice(j * LANES, (j + 1) * LANES)
            keys = pl.ds(pl.multiple_of((base_blk + boff) * kb, kb), kb)
            return _dot_nt(k_ref[keys, hs], q_ref[r * rc:(r + 1) * rc, hs])

        def finish(i, s):
            boff, diag, j, r, mk = work[i]
            qs = slice(r * rc, (r + 1) * rc)
            if mk:
                s = jnp.where(key_i + (diag * kb - r * rc) <= qry_i, s, NEG_BIG)
            m_prev = m_sc[j, :, qs]
            m_new = jnp.maximum(m_prev, jnp.max(s, axis=0, keepdims=True))
            alpha = jnp.exp2(m_prev - m_new)
            p = jnp.exp2(s - _tile_rows(m_new, kb // SUBLANES))
            v_t = vt_ref[base_blk + boff, j * MLA_VT_ROWS:(j + 1) * MLA_VT_ROWS, :]
            pv = _dot(v_t, p.astype(BF16))
            acc_sc[j, :, qs] = _tile_rows(alpha, MLA_VT_ROWS // SUBLANES) * acc_sc[j, :, qs] + pv
            m_sc[j, :, qs] = m_new

        _pipelined(len(work), [scores, finish], [look])

    def full_work(n_blocks):
        return _work_items([(o, None) for o in range(n_blocks)], nrc, rc, kb, strict=False)

    n_full = qi * nd
    n_big = n_full // blocks_per_body

    def body(i, carry):
        region(i * blocks_per_body, full_work(blocks_per_body))
        return carry

    lax.fori_loop(0, n_big, body, 0)
    rem = n_full - n_big * blocks_per_body
    for tiles in range(1, blocks_per_body // nd):
        @pl.when(rem == tiles * nd)
        def _():
            region(n_big * blocks_per_body, full_work(tiles * nd))

    region(n_full, _work_items([(d, d) for d in range(nd)], nrc, rc, kb, strict=False))

    for j in range(2):
        denom = _tile_rows(acc_sc[j, MLA_V:MLA_V + SUBLANES, :], MLA_V // SUBLANES)
        o_ref[j * MLA_V:(j + 1) * MLA_V, :] = (acc_sc[j, :MLA_V, :] / denom).astype(o_ref.dtype)


def _mla_attn(q, k, vt, tq, kb, rc, look, blocks_per_body):
    b, s, _ = q.shape
    hp = MLA_HEADS // 2
    assert blocks_per_body % (tq // kb) == 0 or (tq // kb) % blocks_per_body == 0
    return pl.pallas_call(
        functools.partial(_mla_kernel, tq=tq, kb=kb, rc=rc, look=look, blocks_per_body=blocks_per_body),
        out_shape=jax.ShapeDtypeStruct((b, MLA_HEADS * MLA_V, s), BF16),
        grid=(b, hp, s // tq),
        in_specs=[pl.BlockSpec((None, tq, 2 * LANES), lambda bi, h, i: (bi, i, h)),
                  pl.BlockSpec((None, s, 2 * LANES), lambda bi, h, i: (bi, 0, h)),
                  pl.BlockSpec((None, s // kb, 2 * MLA_VT_ROWS, kb), lambda bi, h, i: (bi, 0, h, 0))],
        out_specs=pl.BlockSpec((None, 2 * MLA_V, tq), lambda bi, h, i: (bi, h, i)),
        scratch_shapes=[pltpu.VMEM((2, SUBLANES, tq), F32), pltpu.VMEM((2, MLA_VT_ROWS, tq), F32)],
        compiler_params=pltpu.CompilerParams(dimension_semantics=("parallel", "parallel", "parallel"),
                                             vmem_limit_bytes=VMEM_LIMIT),
        name="mla_attn",
    )(q, k, vt)


def _sb_kernel(q_ref, k_ref, v_ref, tri_ref, o_ref, qz_sc, r_sc, acc_sc, *, tq, kb, rc, lag_a, lag_b):
    qi = pl.program_id(2)
    nd = tq // kb
    nrc = tq // rc
    rows = lax.broadcasted_iota(jnp.int32, (rc, kb), 0)
    cols = lax.broadcasted_iota(jnp.int32, (rc, kb), 1)
    lane = lax.broadcasted_iota(jnp.int32, (tq, LANES), 1)
    q_both = q_ref[...]
    for j in range(2):
        in_head = (lane >= j * SB_HEAD_DIM) & (lane < (j + 1) * SB_HEAD_DIM)
        qz_sc[j] = jnp.where(in_head, q_both, jnp.zeros_like(q_both))
    r_sc[...] = jnp.zeros(r_sc.shape, F32)
    acc_sc[...] = jnp.zeros(acc_sc.shape, F32)

    def region(base_blk, work, tracked=(None,)):
        r_min = [None] * len(tracked)

        def kv_rows(i):
            return pl.ds(pl.multiple_of((base_blk + work[i][0]) * kb, kb), kb)

        def mask_of(i):
            _boff, diag, _j, r, _mk = work[i]
            return cols + (diag * kb - r * rc) < rows

        def scores(i, _):
            _boff, _diag, j, r, _mk = work[i]
            return _dot_nt(qz_sc[j, r * rc:(r + 1) * rc, :], k_ref[kv_rows(i), :])

        def suffix(i, z):
            _boff, _diag, j, r, mk = work[i]
            rs = slice(r * rc, (r + 1) * rc)
            sp = jnp.where(z > SOFTPLUS_LINEAR_ABOVE, z, jnp.log(1.0 + jnp.exp2(z)) * LOG2E)
            if mk:
                sp = jnp.where(mask_of(i), sp, 0.0)
            c = _dot(sp.astype(BF16), tri_ref[...])
            r_prev = r_sc[j, rs, :]
            r_new = r_prev + jnp.sum(sp, axis=-1, keepdims=True)
            r_sc[j, rs, :] = r_new
            wanted = [n for n, members in enumerate(tracked) if members is None or i in members]
            if wanted:
                parts = [r_new[t8 * SUBLANES:(t8 + 1) * SUBLANES, :] for t8 in range(rc // SUBLANES)]
                while len(parts) > 1:
                    parts = [jnp.minimum(a, b) for a, b in zip(parts[0::2], parts[1::2])]
                for n in wanted:
                    r_min[n] = parts[0] if r_min[n] is None else jnp.minimum(r_min[n], parts[0])
            return z, c, r_prev

        def weights(i, zcr):
            _boff, _diag, j, r, mk = work[i]
            rs = slice(r * rc, (r + 1) * rc)
            z, c, r_prev = zcr
            a = jnp.exp2(z - c - jnp.concatenate([r_prev] * (kb // LANES), axis=1))
            if mk:
                a = jnp.where(mask_of(i), a, 0.0)
            acc_sc[j, rs, :] += _dot(a.astype(BF16), v_ref[kv_rows(i), :])

        _pipelined(len(work), [scores, suffix, weights], [lag_a, lag_b])
        return r_min

    def still_open(low):
        return (jnp.min(low) <= SB_UNDERFLOW_LOG2).astype(jnp.int32)

    diag_items = _work_items([(d, d) for d in reversed(range(nd))], nrc, rc, kb, strict=True)
    own_block = lambda r: ((r + 1) * rc - 1) // kb
    near_items = [it for it in diag_items if it[1] >= own_block(it[3]) - 1]
    far_items = [it for it in diag_items if it[1] < own_block(it[3]) - 1]
    far_chunks = {(it[2], it[3]) for it in far_items}
    n_top = kb // rc
    last_near = [i for i, it in enumerate(near_items) if it[1] == max(own_block(it[3]) - 1, 0)]
    far_deciding = {i for i in last_near if (near_items[i][2], near_items[i][3]) in far_chunks}
    lower_deciding = {i for i in last_near if near_items[i][3] >= n_top}
    far_low, lower_low = region(qi * nd, near_items, tracked=(far_deciding, lower_deciding))
    if far_items:
        @pl.when(still_open(far_low) > 0)
        def _():
            region(qi * nd, far_items, tracked=())

    n_full = qi * nd
    full_work = _work_items([(0, None)], nrc, rc, kb, strict=True)
    top_work = [it for it in full_work if it[3] < n_top]
    lower_work = [it for it in full_work if it[3] >= n_top]

    def cond(carry):
        i, top_open, lower_open = carry
        return jnp.logical_and(i < n_full, (top_open + lower_open) > 0)

    def body(carry):
        i, _, lower_open = carry
        blk = n_full - 1 - i
        top_open = still_open(region(blk, top_work)[0])
        lower_open = lax.cond(lower_open > 0, lambda: still_open(region(blk, lower_work)[0]),
                              lambda: jnp.int32(0))
        return i + 1, top_open, lower_open

    lower_open0 = still_open(lower_low) if lower_work else jnp.int32(0)
    lax.while_loop(cond, body, (jnp.int32(0), jnp.int32(1), lower_open0))
    o_ref[...] = jnp.where(lane < SB_HEAD_DIM, acc_sc[0], acc_sc[1]).astype(o_ref.dtype)


def _sb_attn(q, k, v, tq, kb, rc, lag_a, lag_b):
    b, s, _ = q.shape
    hp = SB_HEADS // 2
    idx = jnp.arange(kb)
    tri = (idx[:, None] >= idx[None, :]).astype(BF16)
    return pl.pallas_call(
        functools.partial(_sb_kernel, tq=tq, kb=kb, rc=rc, lag_a=lag_a, lag_b=lag_b),
        out_shape=jax.ShapeDtypeStruct((b, s, _SBW), BF16),
        grid=(b, hp, s // tq),
        in_specs=[pl.BlockSpec((None, tq, LANES), lambda bi, h, i: (bi, i, h)),
                  pl.BlockSpec((None, s, LANES), lambda bi, h, i: (bi, 0, h)),
                  pl.BlockSpec((None, s, LANES), lambda bi, h, i: (bi, 0, h)),
                  pl.BlockSpec((kb, kb), lambda bi, h, i: (0, 0))],
        out_specs=pl.BlockSpec((None, tq, LANES), lambda bi, h, i: (bi, i, h)),
        scratch_shapes=[pltpu.VMEM((2, tq, LANES), BF16), pltpu.VMEM((2, tq, LANES), F32),
                        pltpu.VMEM((2, tq, LANES), F32)],
        compiler_params=pltpu.CompilerParams(dimension_semantics=("parallel", "parallel", "parallel"),
                                             vmem_limit_bytes=VMEM_LIMIT),
        name="sb_attn",
    )(q, k, v, tri)


def _mem_kv_kernel(mem_ref, g_ref, w_ref, kv_ref):
    mn = _rms(mem_ref[...], g_ref[...]).astype(BF16)
    kv_ref[...] = _dot(mn, w_ref[...]).astype(BF16)


def _mem_kv(mem2, g_mem, w_xkv):
    rows, _ = mem2.shape
    n = w_xkv.shape[1]
    return pl.pallas_call(
        _mem_kv_kernel,
        out_shape=jax.ShapeDtypeStruct((rows, n), BF16),
        grid=(1,),
        in_specs=[_full(mem2.shape), _full(g_mem.shape), _full(w_xkv.shape)],
        out_specs=_full((rows, n)),
        compiler_params=pltpu.CompilerParams(vmem_limit_bytes=VMEM_LIMIT),
        name="mem_kv",
    )(mem2, g_mem, w_xkv)


def _merge_kernel(x_ref, oa_ref, ob_ref, ga_ref, gb_ref, wa_ref, wb_ref, wo_ref, gx_ref, wxq_ref, kv_ref,
                  wxo_ref, y_ref):
    pa = lax.dot_general(oa_ref[...], wa_ref[...], (((0,), (0,)), ((), ())), preferred_element_type=F32)
    merged = ga_ref[...].astype(F32) * pa + gb_ref[...].astype(F32) * _dot(ob_ref[...], wb_ref[...])
    x1 = x_ref[...] + _dot(merged.astype(BF16), wo_ref[...])

    hx = _rms(x1, gx_ref[...]).astype(BF16)
    xw = X_HEADS * X_HEAD_DIM
    xq = (_dot(hx, wxq_ref[...]) * (LOG2E / math.sqrt(X_HEAD_DIM))).astype(BF16)
    heads = []
    for hd in range(X_HEADS):
        sl = slice(hd * X_HEAD_DIM, (hd + 1) * X_HEAD_DIM)
        kh = kv_ref[:, sl]
        vh = kv_ref[:, xw + hd * X_HEAD_DIM: xw + (hd + 1) * X_HEAD_DIM]
        s = _dot_nt(xq[:, sl], kh)
        p = jnp.exp2(s - jnp.max(s, axis=-1, keepdims=True))
        l = jnp.sum(p, axis=-1, keepdims=True)
        heads.append((_dot(p.astype(BF16), vh) / l).astype(BF16))
    xo = jnp.concatenate(heads, axis=1)
    y_ref[...] = x1 + _dot(xo, wxo_ref[...])


def _merge(x2, oa_t, ob, ga, gb, wa, wb, wo, g_x, wxq, kv, wxo, tm, seq, m_len):
    t, d = x2.shape
    per_b = seq // tm
    row = lambda w: pl.BlockSpec((tm, w), lambda i: (i, 0))
    return pl.pallas_call(
        _merge_kernel,
        out_shape=jax.ShapeDtypeStruct((t, d), F32),
        grid=(t // tm,),
        in_specs=[row(d), pl.BlockSpec((None, oa_t.shape[1], tm), lambda i: (i // per_b, 0, i % per_b)),
                  row(ob.shape[1]), row(d), row(d),
                  _full(wa.shape), _full(wb.shape), _full(wo.shape), _full(g_x.shape), _full(wxq.shape),
                  pl.BlockSpec((m_len, kv.shape[1]), lambda i: (i // per_b, 0)),
                  _full(wxo.shape)],
        out_specs=row(d),
        compiler_params=pltpu.CompilerParams(dimension_semantics=("parallel",), vmem_limit_bytes=VMEM_LIMIT),
        name="merge_xattn",
    )(x2, oa_t, ob, ga, gb, wa, wb, wo, g_x, wxq, kv, wxo)


def _ffn_kernel(x_ref, g_ref, wg_ref, wu_ref, wd_ref, gf_ref, y_ref, *, n_chunks):
    x = x_ref[...]
    hf = _rms(x, g_ref[...]).astype(BF16)
    d_ff = wg_ref.shape[1]
    cw = d_ff // n_chunks
    y = x
    for c in range(n_chunks):
        sl = slice(c * cw, (c + 1) * cw)
        g = _dot(hf, wg_ref[:, sl])
        u = _dot(hf, wu_ref[:, sl])
        act = (g * jax.nn.sigmoid(g) * u).astype(BF16)
        y = y + _dot(act, wd_ref[sl, :])
    y_ref[...] = _rms(y, gf_ref[...])


def _ffn(x2, g_ffn, wg, wu, wd, g_final, tm, n_chunks):
    t, d = x2.shape
    row = pl.BlockSpec((tm, d), lambda i: (i, 0))
    return pl.pallas_call(
        functools.partial(_ffn_kernel, n_chunks=n_chunks),
        out_shape=jax.ShapeDtypeStruct((t, d), F32),
        grid=(t // tm,),
        in_specs=[row, _full(g_ffn.shape), _full(wg.shape), _full(wu.shape), _full(wd.shape), _full(g_final.shape)],
        out_specs=row,
        compiler_params=pltpu.CompilerParams(dimension_semantics=("parallel",), vmem_limit_bytes=VMEM_LIMIT),
        name="ffn",
    )(x2, g_ffn, wg, wu, wd, g_final)


def kernel(x, mem, positions, g_mix, w_in, b_gate, g_q_lat, w_uq, g_kv_lat, w_ukv, w_a_proj, w_b_proj, w_o,
           g_x, g_mem, w_xq, w_xkv, w_xo, g_ffn, w_gate, w_up, w_down, g_final):
    b, s, d = x.shape
    m_len = mem.shape[1]
    assert g_mix.shape[0] == 1, "the final RMSNorm is fused into the (single) layer's FFN kernel"
    tm = min(1024, s)
    tm_in = min(1024, s)
    tq = min(1024, s)
    kb = min(256, s)
    rc = min(128, s)
    t = b * s

    inv_freq = ROPE_THETA ** (-jnp.arange(0, MLA_ROPE, 2, dtype=F32) / MLA_ROPE)
    invf = jnp.tile(inv_freq, 2 * _POS_GROUPS)[None, :]
    pos_c = positions.reshape(t // tm_in, _POS_GROUPS, tm_in // _POS_GROUPS).transpose(0, 2, 1)
    pos_c = jnp.repeat(pos_c.reshape(t // _POS_GROUPS, _POS_GROUPS), MLA_ROPE, axis=1)
    x2 = x.reshape(t, d)
    mem2 = mem.reshape(b * m_len, d)

    w_head, w_tail, wuq, wk, wv = _pack_in_weights(w_in[0], w_uq[0], w_ukv[0], d)
    q, k, vt, sbq, sbk, sbv, ga, gb = _in_proj(
        x2, pos_c, invf, g_mix[0][None, :], w_head, w_tail, b_gate[0], g_q_lat[0][None, :], wuq,
        g_kv_lat[0][None, :], wk, wv, tm_in, kb)
    o_a_t = _mla_attn(q.reshape(b, s, -1), k.reshape(b, s, -1), vt.reshape(b, s // kb, -1, kb), min(2048, s), kb,
                      kb, look=8, blocks_per_body=8)
    o_b = _sb_attn(sbq.reshape(b, s, -1), sbk.reshape(b, s, -1), sbv.reshape(b, s, -1), min(2048, s), kb, rc,
                   lag_a=3, lag_b=3)
    kv = _mem_kv(mem2, g_mem[0][None, :], w_xkv[0].astype(BF16))
    x2 = _merge(x2, o_a_t, o_b.reshape(t, -1), ga, gb,
                w_a_proj[0].astype(BF16), w_b_proj[0].astype(BF16), w_o[0].astype(BF16),
                g_x[0][None, :], w_xq[0].astype(BF16), kv, w_xo[0].astype(BF16), tm, s, m_len)
    x2 = _ffn(x2, g_ffn[0][None, :], w_gate[0].astype(BF16), w_up[0].astype(BF16), w_down[0].astype(BF16),
              g_final[None, :], tm, 1)
    return x2.reshape(b, s, d)
```

```python
import functools
import math

import jax
import jax.numpy as jnp
from jax import lax
from jax.experimental import pallas as pl
from jax.experimental.pallas import tpu as pltpu

F32 = jnp.float32
BF16 = jnp.bfloat16

EPS = 1e-6
ROPE_THETA = 10000.0
LOG2E = math.log2(math.e)

MLA_HEADS = 8
MLA_Q_RANK = 256
MLA_KV_RANK = 128
MLA_NOPE = 64
MLA_ROPE = 32
MLA_V = 64
SB_HEADS = 8
SB_HEAD_DIM = 64
X_HEADS = 4
X_HEAD_DIM = 128
LANES = 128
SUBLANES = 8
BF16_ROWS = 16
MLA_VT_ROWS = MLA_V + BF16_ROWS
NEG_BIG = -1e30
SB_UNDERFLOW_LOG2 = 160.0
SOFTPLUS_LINEAR_ABOVE = 64.0

VMEM_LIMIT = 56 * 1024 * 1024


def _rms(x, g):
    ms = jnp.mean(x * x, axis=-1, keepdims=True)
    return x * lax.rsqrt(ms + EPS) * g


def _dot(a, b):
    return jnp.dot(a, b, preferred_element_type=F32)


def _dot_nt(a, b):
    return lax.dot_general(a, b, (((1,), (1,)), ((), ())), preferred_element_type=F32)


def _pipelined(n_items, stages, lags):
    offs = [0]
    for lag in lags:
        offs.append(offs[-1] + lag)
    vals = [dict() for _ in stages]
    for tick in range(n_items + offs[-1]):
        for k, stage in enumerate(stages):
            i = tick - offs[k]
            if 0 <= i < n_items:
                prev = vals[k - 1].pop(i) if k > 0 else None
                vals[k][i] = stage(i, prev)


_C_CQ = 0
_C_CKV = _C_CQ + MLA_Q_RANK
_C_KRX = _C_CKV + MLA_KV_RANK
_SBW = SB_HEADS * SB_HEAD_DIM
_T_SBQ = 0
_T_SBK = _T_SBQ + _SBW
_T_SBV = _T_SBK + _SBW
_T_GA = _T_SBV + _SBW
_ROPE_END = MLA_NOPE + MLA_ROPE
_POS_GROUPS = LANES // MLA_ROPE


def _in_proj_kernel(x_ref, pos_ref, invf_ref, gmix_ref, whead_ref, wtail_ref, bg_ref, gq_ref, wuq_ref,
                    gkv_ref, wk_ref, wv_ref,
                    q_ref, k_ref, vt_ref, sbq_ref, sbk_ref, sbv_ref, ga_ref, gb_ref, *, d_model):
    x = x_ref[...]
    h = _rms(x, gmix_ref[...]).astype(BF16)

    def proj(c0, width, w_ref=wtail_ref):
        return _dot(h, w_ref[:, c0:c0 + width])

    cq_raw = proj(_C_CQ, MLA_Q_RANK, whead_ref)
    ckv_krx = proj(_C_CKV, MLA_KV_RANK + LANES, whead_ref)
    sbq_ref[...] = (proj(_T_SBQ, _SBW) * (LOG2E / math.sqrt(SB_HEAD_DIM))).astype(BF16)
    sbk_ref[...] = proj(_T_SBK, _SBW).astype(BF16)

    cq = _rms(cq_raw, gq_ref[...]).astype(BF16)
    ckv = _rms(ckv_krx[:, :MLA_KV_RANK], gkv_ref[...]).astype(BF16)
    qt = _dot(cq, wuq_ref[...])
    kn = _dot(ckv, wk_ref[...])
    vv = _dot(ckv, wv_ref[...])
    gate_a = proj(_T_GA, d_model)
    gate_b = proj(_T_GA + d_model, d_model)
    sbv_ref[...] = proj(_T_SBV, _SBW).astype(BF16)

    lane = lax.broadcasted_iota(jnp.int32, (1, LANES), 1)
    ang = pos_ref[...].astype(F32) * invf_ref[...]
    cos_c = jnp.cos(ang)
    sin_c = jnp.sin(ang)
    is_rope = (lane >= MLA_NOPE) & (lane < _ROPE_END)
    cos_parts, sin_parts = [], []
    for g in range(_POS_GROUPS):
        shift = (MLA_NOPE - g * MLA_ROPE) % LANES
        cos_g = cos_c if shift == 0 else pltpu.roll(cos_c, shift, axis=1)
        sin_g = sin_c if shift == 0 else pltpu.roll(sin_c, shift, axis=1)
        cos_parts.append(jnp.where(is_rope, cos_g, jnp.where(lane < MLA_NOPE, 1.0, 0.0)))
        sin_parts.append(jnp.where(is_rope, sin_g, 0.0))
    cos = jnp.concatenate(cos_parts, axis=0)
    sin = jnp.concatenate(sin_parts, axis=0)
    partner_shift = LANES - MLA_ROPE

    def rotary(t, c, s):
        return t * c + pltpu.roll(t, partner_shift, axis=1) * s

    q_scale = LOG2E / math.sqrt(MLA_NOPE + MLA_ROPE)
    cos_q = cos * q_scale
    sin_q = sin * q_scale
    for hd in range(MLA_HEADS):
        sl = slice(hd * LANES, (hd + 1) * LANES)
        q_ref[:, sl] = rotary(qt[:, sl], cos_q, sin_q).astype(BF16)

    krope = rotary(ckv_krx[:, MLA_KV_RANK:], cos, sin)
    for hd in range(MLA_HEADS):
        sl = slice(hd * LANES, (hd + 1) * LANES)
        k_ref[:, sl] = (kn[:, sl] + krope).astype(BF16)

    v_lane = lax.broadcasted_iota(jnp.int32, (1, MLA_HEADS * LANES), 1)
    ones_cols = jnp.where(v_lane % LANES >= MLA_V, 1.0, 0.0).astype(F32)
    vv = vv + ones_cols
    n_kblk, _, kb = vt_ref.shape
    for c in range(n_kblk):
        for hd in range(MLA_HEADS):
            v_t = vv[c * kb:(c + 1) * kb, hd * LANES:(hd + 1) * LANES].T
            vt_ref[c, hd * MLA_VT_ROWS:(hd + 1) * MLA_VT_ROWS, :] = v_t[:MLA_VT_ROWS].astype(BF16)

    ga_ref[...] = jax.nn.sigmoid(gate_a + bg_ref[0:1, :]).astype(BF16)
    gb_ref[...] = jax.nn.sigmoid(gate_b + bg_ref[1:2, :]).astype(BF16)


def _pack_in_weights(w_in, w_uq, w_ukv, d_model):
    f = w_in.dtype
    zeros = lambda n: jnp.zeros((d_model, n), f)
    o_kr = MLA_Q_RANK + MLA_KV_RANK
    o_sb = o_kr + MLA_ROPE
    half = MLA_ROPE // 2
    assert LANES - MLA_NOPE - MLA_ROPE == MLA_ROPE
    kr = w_in[:, o_kr:o_kr + MLA_ROPE]
    kr_rot = jnp.concatenate([-kr[:, half:], kr[:, :half]], axis=1)
    w_head = jnp.concatenate([w_in[:, :o_kr], zeros(MLA_NOPE), kr, kr_rot], axis=1).astype(BF16)
    w_tail = w_in[:, o_sb:].astype(BF16)

    r = w_uq.shape[0]
    wq = w_uq.reshape(r, MLA_HEADS, MLA_NOPE + MLA_ROPE)
    rope = wq[:, :, MLA_NOPE:]
    rope_rot = jnp.concatenate([-rope[:, :, half:], rope[:, :, :half]], axis=2)
    wuq = jnp.concatenate([wq, rope_rot], axis=2).reshape(r, MLA_HEADS * LANES).astype(BF16)

    rk = w_ukv.shape[0]
    wkv = w_ukv.reshape(rk, MLA_HEADS, MLA_NOPE + MLA_V)
    wk = jnp.concatenate([wkv[:, :, :MLA_NOPE], jnp.zeros((rk, MLA_HEADS, LANES - MLA_NOPE), f)], axis=2)
    wk = wk.reshape(rk, MLA_HEADS * LANES).astype(BF16)
    wv = jnp.concatenate([wkv[:, :, MLA_NOPE:], jnp.zeros((rk, MLA_HEADS, LANES - MLA_V), f)], axis=2)
    wv = wv.reshape(rk, MLA_HEADS * LANES).astype(BF16)
    return w_head, w_tail, wuq, wk, wv


def _full(shape):
    n = len(shape)
    return pl.BlockSpec(shape, lambda *_: (0,) * n, pipeline_mode=pl.Buffered(1))


def _in_proj(x2, pos_c, invf, g_mix, w_head, w_tail, b_gate, g_q, wuq, g_kv, wk, wv, tm, kb):
    t, d = x2.shape
    row = lambda w: pl.BlockSpec((tm, w), lambda i: (i, 0))
    pos_spec = pl.BlockSpec((tm // _POS_GROUPS, LANES), lambda i: (i, 0))
    hw = MLA_HEADS * LANES
    out_widths = [hw, hw, None, _SBW, _SBW, _SBW, d, d]
    vt_shape = jax.ShapeDtypeStruct((t // kb, MLA_HEADS * MLA_VT_ROWS, kb), BF16)
    vt_spec = pl.BlockSpec((tm // kb, MLA_HEADS * MLA_VT_ROWS, kb), lambda i: (i, 0, 0))
    return pl.pallas_call(
        functools.partial(_in_proj_kernel, d_model=d),
        out_shape=[vt_shape if w is None else jax.ShapeDtypeStruct((t, w), BF16) for w in out_widths],
        grid=(t // tm,),
        in_specs=[row(d), pos_spec, _full(invf.shape), _full(g_mix.shape), _full(w_head.shape), _full(w_tail.shape),
                  _full(b_gate.shape), _full(g_q.shape), _full(wuq.shape),
                  _full(g_kv.shape), _full(wk.shape), _full(wv.shape)],
        out_specs=[vt_spec if w is None else row(w) for w in out_widths],
        compiler_params=pltpu.CompilerParams(dimension_semantics=("parallel",), vmem_limit_bytes=VMEM_LIMIT),
        name="in_proj",
    )(x2, pos_c, invf, g_mix, w_head, w_tail, b_gate, g_q, wuq, g_kv, wk, wv)


def _work_items(blocks, n_chunks, rc, kb, strict):
    items = []
    for boff, diag in blocks:
        for j in range(2):
            for r in range(n_chunks):
                if diag is None:
                    items.append((boff, diag, j, r, False))
                    continue
                lo_row, hi_row = r * rc, (r + 1) * rc - 1
                lo_col, hi_col = diag * kb, (diag + 1) * kb - 1
                if (lo_col >= hi_row) if strict else (lo_col > hi_row):
                    continue
                fully_visible = (hi_col < lo_row) if strict else (hi_col <= lo_row)
                items.append((boff, diag, j, r, not fully_visible))
    return items


def _tile_rows(x, n):
    return jnp.concatenate([x] * n, axis=0)


def _mla_kernel(q_ref, k_ref, vt_ref, o_ref, m_sc, acc_sc, *, tq, kb, rc, look, blocks_per_body):
    qi = pl.program_id(2)
    nd = tq // kb
    nrc = tq // rc
    key_i = lax.broadcasted_iota(jnp.int32, (kb, rc), 0)
    qry_i = lax.broadcasted_iota(jnp.int32, (kb, rc), 1)
    m_sc[...] = jnp.full(m_sc.shape, NEG_BIG, F32)
    acc_sc[...] = jnp.zeros(acc_sc.shape, F32)

    def region(base_blk, work):
        def scores(i, _):
            boff, _diag, j, r, _mk = work[i]
            hs = slice(j * LANES, (j + 1) * LANES)
            keys = pl.ds(pl.multiple_of((base_blk + boff) * kb, kb), kb)
            return _dot_nt(k_ref[keys, hs], q_ref[r * rc:(r + 1) * rc, hs])

        def finish(i, s):
            boff, diag, j, r, mk = work[i]
            qs = slice(r * rc, (r + 1) * rc)
            if mk:
                s = jnp.where(key_i + (diag * kb - r * rc) <= qry_i, s, NEG_BIG)
            m_prev = m_sc[j, :, qs]
            m_new = jnp.maximum(m_prev, jnp.max(s, axis=0, keepdims=True))
            alpha = jnp.exp2(m_prev - m_new)
            p = jnp.exp2(s - _tile_rows(m_new, kb // SUBLANES))
            v_t = vt_ref[base_blk + boff, j * MLA_VT_ROWS:(j + 1) * MLA_VT_ROWS, :]
            pv = _dot(v_t, p.astype(BF16))
            acc_sc[j, :, qs] = _tile_rows(alpha, MLA_VT_ROWS // SUBLANES) * acc_sc[j, :, qs] + pv
            m_sc[j, :, qs] = m_new

        _pipelined(len(work), [scores, finish], [look])

    def full_work(n_blocks):
        return _work_items([(o, None) for o in range(n_blocks)], nrc, rc, kb, strict=False)

    n_full = qi * nd
    n_big = n_full // blocks_per_body

    def body(i, carry):
        region(i * blocks_per_body, full_work(blocks_per_body))
        return carry

    lax.fori_loop(0, n_big, body, 0)
    rem = n_full - n_big * blocks_per_body
    for tiles in range(1, blocks_per_body // nd):
        @pl.when(rem == tiles * nd)
        def _():
            region(n_big * blocks_per_body, full_work(tiles * nd))

    region(n_full, _work_items([(d, d) for d in range(nd)], nrc, rc, kb, strict=False))

    for j in range(2):
        denom = _tile_rows(acc_sc[j, MLA_V:MLA_V + SUBLANES, :], MLA_V // SUBLANES)
        o_ref[j * MLA_V:(j + 1) * MLA_V, :] = (acc_sc[j, :MLA_V, :] / denom).astype(o_ref.dtype)


def _mla_attn(q, k, vt, tq, kb, rc, look, blocks_per_body):
    b, s, _ = q.shape
    hp = MLA_HEADS // 2
    assert blocks_per_body % (tq // kb) == 0 or (tq // kb) % blocks_per_body == 0
    return pl.pallas_call(
        functools.partial(_mla_kernel, tq=tq, kb=kb, rc=rc, look=look, blocks_per_body=blocks_per_body),
        out_shape=jax.ShapeDtypeStruct((b, MLA_HEADS * MLA_V, s), BF16),
        grid=(b, hp, s // tq),
        in_specs=[pl.BlockSpec((None, tq, 2 * LANES), lambda bi, h, i: (bi, i, h)),
                  pl.BlockSpec((None, s, 2 * LANES), lambda bi, h, i: (bi, 0, h)),
                  pl.BlockSpec((None, s // kb, 2 * MLA_VT_ROWS, kb), lambda bi, h, i: (bi, 0, h, 0))],
        out_specs=pl.BlockSpec((None, 2 * MLA_V, tq), lambda bi, h, i: (bi, h, i)),
        scratch_shapes=[pltpu.VMEM((2, SUBLANES, tq), F32), pltpu.VMEM((2, MLA_VT_ROWS, tq), F32)],
        compiler_params=pltpu.CompilerParams(dimension_semantics=("parallel", "parallel", "parallel"),
                                             vmem_limit_bytes=VMEM_LIMIT),
        name="mla_attn",
    )(q, k, vt)


def _sb_kernel(q_ref, k_ref, v_ref, tri_ref, o_ref, qz_sc, r_sc, acc_sc, *, tq, kb, rc, lag_a, lag_b):
    qi = pl.program_id(2)
    nd = tq // kb
    nrc = tq // rc
    rows = lax.broadcasted_iota(jnp.int32, (rc, kb), 0)
    cols = lax.broadcasted_iota(jnp.int32, (rc, kb), 1)
    lane = lax.broadcasted_iota(jnp.int32, (tq, LANES), 1)
    q_both = q_ref[...]
    for j in range(2):
        in_head = (lane >= j * SB_HEAD_DIM) & (lane < (j + 1) * SB_HEAD_DIM)
        qz_sc[j] = jnp.where(in_head, q_both, jnp.zeros_like(q_both))
    r_sc[...] = jnp.zeros(r_sc.shape, F32)
    acc_sc[...] = jnp.zeros(acc_sc.shape, F32)

    def region(base_blk, work, tracked=(None,)):
        r_min = [None] * len(tracked)

        def kv_rows(i):
            return pl.ds(pl.multiple_of((base_blk + work[i][0]) * kb, kb), kb)

        def mask_of(i):
            _boff, diag, _j, r, _mk = work[i]
            return cols + (diag * kb - r * rc) < rows

        def scores(i, _):
            _boff, _diag, j, r, _mk = work[i]
            return _dot_nt(qz_sc[j, r * rc:(r + 1) * rc, :], k_ref[kv_rows(i), :])

        def suffix(i, z):
            _boff, _diag, j, r, mk = work[i]
            rs = slice(r * rc, (r + 1) * rc)
            sp = jnp.where(z > SOFTPLUS_LINEAR_ABOVE, z, jnp.log(1.0 + jnp.exp2(z)) * LOG2E)
            if mk:
                sp = jnp.where(mask_of(i), sp, 0.0)
            c = _dot(sp.astype(BF16), tri_ref[...])
            r_prev = r_sc[j, rs, :]
            r_new = r_prev + jnp.sum(sp, axis=-1, keepdims=True)
            r_sc[j, rs, :] = r_new
            wanted = [n for n, members in enumerate(tracked) if members is None or i in members]
            if wanted:
                parts = [r_new[t8 * SUBLANES:(t8 + 1) * SUBLANES, :] for t8 in range(rc // SUBLANES)]
                while len(parts) > 1:
                    parts = [jnp.minimum(a, b) for a, b in zip(parts[0::2], parts[1::2])]
                for n in wanted:
                    r_min[n] = parts[0] if r_min[n] is None else jnp.minimum(r_min[n], parts[0])
            return z, c, r_prev

        def weights(i, zcr):
            _boff, _diag, j, r, mk = work[i]
            rs = slice(r * rc, (r + 1) * rc)
            z, c, r_prev = zcr
            a = jnp.exp2(z - c - jnp.concatenate([r_prev] * (kb // LANES), axis=1))
            if mk:
                a = jnp.where(mask_of(i), a, 0.0)
            acc_sc[j, rs, :] += _dot(a.astype(BF16), v_ref[kv_rows(i), :])

        _pipelined(len(work), [scores, suffix, weights], [lag_a, lag_b])
        return r_min

    def still_open(low):
        return (jnp.min(low) <= SB_UNDERFLOW_LOG2).astype(jnp.int32)

    diag_items = _work_items([(d, d) for d in reversed(range(nd))], nrc, rc, kb, strict=True)
    own_block = lambda r: ((r + 1) * rc - 1) // kb
    near_items = [it for it in diag_items if it[1] >= own_block(it[3]) - 1]
    far_items = [it for it in diag_items if it[1] < own_block(it[3]) - 1]
    far_chunks = {(it[2], it[3]) for it in far_items}
    n_top = kb // rc
    last_near = [i for i, it in enumerate(near_items) if it[1] == max(own_block(it[3]) - 1, 0)]
    far_deciding = {i for i in last_near if (near_items[i][2], near_items[i][3]) in far_chunks}
    lower_deciding = {i for i in last_near if near_items[i][3] >= n_top}
    far_low, lower_low = region(qi * nd, near_items, tracked=(far_deciding, lower_deciding))
    if far_items:
        @pl.when(still_open(far_low) > 0)
        def _():
            region(qi * nd, far_items, tracked=())

    n_full = qi * nd
    full_work = _work_items([(0, None)], nrc, rc, kb, strict=True)
    top_work = [it for it in full_work if it[3] < n_top]
    lower_work = [it for it in full_work if it[3] >= n_top]

    def cond(carry):
        i, top_open, lower_open = carry
        return jnp.logical_and(i < n_full, (top_open + lower_open) > 0)

    def body(carry):
        i, _, lower_open = carry
        blk = n_full - 1 - i
        top_open = still_open(region(blk, top_work)[0])
        lower_open = lax.cond(lower_open > 0, lambda: still_open(region(blk, lower_work)[0]),
                              lambda: jnp.int32(0))
        return i + 1, top_open, lower_open

    lower_open0 = still_open(lower_low) if lower_work else jnp.int32(0)
    lax.while_loop(cond, body, (jnp.int32(0), jnp.int32(1), lower_open0))
    o_ref[...] = jnp.where(lane < SB_HEAD_DIM, acc_sc[0], acc_sc[1]).astype(o_ref.dtype)


def _sb_attn(q, k, v, tq, kb, rc, lag_a, lag_b):
    b, s, _ = q.shape
    hp = SB_HEADS // 2
    idx = jnp.arange(kb)
    tri = (idx[:, None] >= idx[None, :]).astype(BF16)
    return pl.pallas_call(
        functools.partial(_sb_kernel, tq=tq, kb=kb, rc=rc, lag_a=lag_a, lag_b=lag_b),
        out_shape=jax.ShapeDtypeStruct((b, s, _SBW), BF16),
        grid=(b, hp, s // tq),
        in_specs=[pl.BlockSpec((None, tq, LANES), lambda bi, h, i: (bi, i, h)),
                  pl.BlockSpec((None, s, LANES), lambda bi, h, i: (bi, 0, h)),
                  pl.BlockSpec((None, s, LANES), lambda bi, h, i: (bi, 0, h)),
                  pl.BlockSpec((kb, kb), lambda bi, h, i: (0, 0))],
        out_specs=pl.BlockSpec((None, tq, LANES), lambda bi, h, i: (bi, i, h)),
        scratch_shapes=[pltpu.VMEM((2, tq, LANES), BF16), pltpu.VMEM((2, tq, LANES), F32),
                        pltpu.VMEM((2, tq, LANES), F32)],
        compiler_params=pltpu.CompilerParams(dimension_semantics=("parallel", "parallel", "parallel"),
                                             vmem_limit_bytes=VMEM_LIMIT),
        name="sb_attn",
    )(q, k, v, tri)


def _merge_kernel(x_ref, oa_ref, ob_ref, ga_ref, gb_ref, wa_ref, wb_ref, wo_ref, gx_ref, wxq_ref, mem_ref,
                  gmem_ref, wxkv_ref, wxo_ref, y_ref, kv_ref, *, tiles_per_batch):
    @pl.when(pl.program_id(0) % tiles_per_batch == 0)
    def _():
        mn = _rms(mem_ref[...], gmem_ref[...]).astype(BF16)
        kv_ref[...] = _dot(mn, wxkv_ref[...]).astype(BF16)

    pa = lax.dot_general(oa_ref[...], wa_ref[...], (((0,), (0,)), ((), ())), preferred_element_type=F32)
    merged = ga_ref[...].astype(F32) * pa + gb_ref[...].astype(F32) * _dot(ob_ref[...], wb_ref[...])
    x1 = x_ref[...] + _dot(merged.astype(BF16), wo_ref[...])

    hx = _rms(x1, gx_ref[...]).astype(BF16)
    xw = X_HEADS * X_HEAD_DIM
    xq = (_dot(hx, wxq_ref[...]) * (LOG2E / math.sqrt(X_HEAD_DIM))).astype(BF16)
    heads = []
    for hd in range(X_HEADS):
        sl = slice(hd * X_HEAD_DIM, (hd + 1) * X_HEAD_DIM)
        kh = kv_ref[:, sl]
        vh = kv_ref[:, xw + hd * X_HEAD_DIM: xw + (hd + 1) * X_HEAD_DIM]
        s = _dot_nt(xq[:, sl], kh)
        p = jnp.exp2(s - jnp.max(s, axis=-1, keepdims=True))
        l = jnp.sum(p, axis=-1, keepdims=True)
        heads.append((_dot(p.astype(BF16), vh) / l).astype(BF16))
    xo = jnp.concatenate(heads, axis=1)
    y_ref[...] = x1 + _dot(xo, wxo_ref[...])


def _merge(x2, oa_t, ob, ga, gb, wa, wb, wo, g_x, wxq, mem2, g_mem, wxkv, wxo, tm, seq, m_len):
    t, d = x2.shape
    per_b = seq // tm
    row = lambda w: pl.BlockSpec((tm, w), lambda i: (i, 0))
    return pl.pallas_call(
        functools.partial(_merge_kernel, tiles_per_batch=per_b),
        out_shape=jax.ShapeDtypeStruct((t, d), F32),
        grid=(t // tm,),
        in_specs=[row(d), pl.BlockSpec((None, oa_t.shape[1], tm), lambda i: (i // per_b, 0, i % per_b)),
                  row(ob.shape[1]), row(d), row(d),
                  _full(wa.shape), _full(wb.shape), _full(wo.shape), _full(g_x.shape), _full(wxq.shape),
                  pl.BlockSpec((m_len, d), lambda i: (i // per_b, 0)),
                  _full(g_mem.shape), _full(wxkv.shape), _full(wxo.shape)],
        out_specs=row(d),
        scratch_shapes=[pltpu.VMEM((m_len, wxkv.shape[1]), BF16)],
        compiler_params=pltpu.CompilerParams(dimension_semantics=("arbitrary",), vmem_limit_bytes=VMEM_LIMIT),
        name="merge_xattn",
    )(x2, oa_t, ob, ga, gb, wa, wb, wo, g_x, wxq, mem2, g_mem, wxkv, wxo)


def _ffn_kernel(x_ref, g_ref, wg_ref, wu_ref, wd_ref, gf_ref, y_ref, *, n_chunks):
    x = x_ref[...]
    hf = _rms(x, g_ref[...]).astype(BF16)
    d_ff = wg_ref.shape[1]
    cw = d_ff // n_chunks
    y = x
    for c in range(n_chunks):
        sl = slice(c * cw, (c + 1) * cw)
        g = _dot(hf, wg_ref[:, sl])
        u = _dot(hf, wu_ref[:, sl])
        act = (g * jax.nn.sigmoid(g) * u).astype(BF16)
        y = y + _dot(act, wd_ref[sl, :])
    y_ref[...] = _rms(y, gf_ref[...])


def _ffn(x2, g_ffn, wg, wu, wd, g_final, tm, n_chunks):
    t, d = x2.shape
    row = pl.BlockSpec((tm, d), lambda i: (i, 0))
    return pl.pallas_call(
        functools.partial(_ffn_kernel, n_chunks=n_chunks),
        out_shape=jax.ShapeDtypeStruct((t, d), F32),
        grid=(t // tm,),
        in_specs=[row, _full(g_ffn.shape), _full(wg.shape), _full(wu.shape), _full(wd.shape), _full(g_final.shape)],
        out_specs=row,
        compiler_params=pltpu.CompilerParams(dimension_semantics=("parallel",), vmem_limit_bytes=VMEM_LIMIT),
        name="ffn",
    )(x2, g_ffn, wg, wu, wd, g_final)


def kernel(x, mem, positions, g_mix, w_in, b_gate, g_q_lat, w_uq, g_kv_lat, w_ukv, w_a_proj, w_b_proj, w_o,
           g_x, g_mem, w_xq, w_xkv, w_xo, g_ffn, w_gate, w_up, w_down, g_final):
    b, s, d = x.shape
    m_len = mem.shape[1]
    assert g_mix.shape[0] == 1, "the final RMSNorm is fused into the (single) layer's FFN kernel"
    tm = min(1024, s)
    tm_in = min(1024, s)
    tq = min(1024, s)
    kb = min(256, s)
    rc = min(128, s)
    t = b * s

    inv_freq = ROPE_THETA ** (-jnp.arange(0, MLA_ROPE, 2, dtype=F32) / MLA_ROPE)
    invf = jnp.tile(inv_freq, 2 * _POS_GROUPS)[None, :]
    pos_c = positions.reshape(t // tm_in, _POS_GROUPS, tm_in // _POS_GROUPS).transpose(0, 2, 1)
    pos_c = jnp.repeat(pos_c.reshape(t // _POS_GROUPS, _POS_GROUPS), MLA_ROPE, axis=1)
    x2 = x.reshape(t, d)
    mem2 = mem.reshape(b * m_len, d)

    w_head, w_tail, wuq, wk, wv = _pack_in_weights(w_in[0], w_uq[0], w_ukv[0], d)
    q, k, vt, sbq, sbk, sbv, ga, gb = _in_proj(
        x2, pos_c, invf, g_mix[0][None, :], w_head, w_tail, b_gate[0], g_q_lat[0][None, :], wuq,
        g_kv_lat[0][None, :], wk, wv, tm_in, kb)
    o_a_t = _mla_attn(q.reshape(b, s, -1), k.reshape(b, s, -1), vt.reshape(b, s // kb, -1, kb), min(2048, s), kb,
                      kb, look=8, blocks_per_body=8)
    o_b = _sb_attn(sbq.reshape(b, s, -1), sbk.reshape(b, s, -1), sbv.reshape(b, s, -1), min(2048, s), kb, rc,
                   lag_a=3, lag_b=3)
    x2 = _merge(x2, o_a_t, o_b.reshape(t, -1), ga, gb,
                w_a_proj[0].astype(BF16), w_b_proj[0].astype(BF16), w_o[0].astype(BF16),
                g_x[0][None, :], w_xq[0].astype(BF16), mem2, g_mem[0][None, :], w_xkv[0].astype(BF16),
                w_xo[0].astype(BF16), tm, s, m_len)
    x2 = _ffn(x2, g_ffn[0][None, :], w_gate[0].astype(BF16), w_up[0].astype(BF16), w_down[0].astype(BF16),
              g_final[None, :], tm, 1)
    return x2.reshape(b, s, d)
```
